```python
import math
import jax, jax.numpy as jnp
from jax import lax
import numpy as np

D_MODEL = 2048
BATCH = 1
SEQ = 8192
DEPTH = 4

GRID_W = 64
CTX_LEN = 256
HEAD_DIM = 128
ROPE_THETA = 10000.0
RMS_EPS = 1e-6
N_MOD = 6
MLP_HIDDEN = 4 * D_MODEL
SSM_INNER = D_MODEL
SSM_HEAD_DIM = 64
SSM_HEADS = SSM_INNER // SSM_HEAD_DIM
SSM_GROUPS = 8
SSM_STATE = 128
SSM_NORM_GROUPS = 8
CONV_K = 5
SSD_CHUNK = 128
CONV_DIM = SSM_INNER + 2 * SSM_GROUPS * SSM_STATE
NA_WIDTH = D_MODEL // 2
NA_HEADS = NA_WIDTH // HEAD_DIM
NA_ROWS = 8
NA_COLS = 16
AB_IN = SSM_INNER + CONV_DIM + 2 * SSM_HEADS + 3 * NA_WIDTH
AB_OUT = SSM_INNER + NA_WIDTH
ATTN_HEADS = D_MODEL // HEAD_DIM
KV_HEADS = ATTN_HEADS // 4
Q_BLOCK = 128
C_QKV = (ATTN_HEADS + 2 * KV_HEADS) * HEAD_DIM

kernel_name = 'hybrid_ssd_natten_gqa_diffusion_trunk'

F32 = jnp.float32


def _rms(u, w):
    uf = u.astype(F32)
    y = uf * lax.rsqrt(jnp.mean(uf * uf, axis=-1, keepdims=True) + RMS_EPS)
    return (y * w.astype(F32)).astype(u.dtype)


def _mlp(h, w1, w2):
    a = jax.nn.relu(h @ w1)
    return (a * a) @ w2


def _rope_tables(n, dtype):
    t = jnp.arange(n)
    half = HEAD_DIM // 2
    inv = 1.0 / (ROPE_THETA ** (jnp.arange(0, half, 2, dtype=F32) / half))
    ang = jnp.concatenate([(t // GRID_W).astype(F32)[:, None] * inv,
                           (t % GRID_W).astype(F32)[:, None] * inv], axis=-1)
    return jnp.cos(ang).astype(dtype), jnp.sin(ang).astype(dtype)


def _rope(u, cos, sin):
    pairs = u.reshape(u.shape[:-1] + (HEAD_DIM // 2, 2))
    u1, u2 = pairs[..., 0], pairs[..., 1]
    c = cos[None, :, None, :]
    s = sin[None, :, None, :]
    return jnp.stack([u1 * c - u2 * s, u1 * s + u2 * c], axis=-1).reshape(u.shape)


def _attend_blocks(q, k, v):
    b, lq, hq, d = q.shape
    hkv = k.shape[2]
    g = hq // hkv
    scale = d ** -0.5
    qb = q.reshape(b, lq // Q_BLOCK, Q_BLOCK, hkv, g, d).swapaxes(0, 1)

    def one_block(qi):
        s = jnp.einsum('bqkgd,bskd->bkgqs', qi, k).astype(F32) * scale
        p = jax.nn.softmax(s, axis=-1).astype(v.dtype)
        return jnp.einsum('bkgqs,bskd->bqkgd', p, v)

    out = lax.map(one_block, qb)
    return out.swapaxes(0, 1).reshape(b, lq, hq * d)


def _neighbourhood_attention(q, k, v, k_ctx, v_ctx, rpb, rows):
    b, n, h, d = q.shape
    kr = min(NA_ROWS, rows)
    scale = d ** -0.5
    r = jnp.arange(rows)
    row_idx = jnp.clip(r - kr // 2, 0, rows - kr)[:, None] + jnp.arange(kr)[None, :]
    col = jnp.arange(GRID_W)
    c0 = jnp.clip(col - NA_COLS // 2, 0, GRID_W - NA_COLS)
    col_mask = (col[None, :] >= c0[:, None]) & (col[None, :] < c0[:, None] + NA_COLS)
    qg = q.reshape(b, rows, GRID_W, h, d)
    kg = k.reshape(b, rows, GRID_W, h, d)[:, row_idx]
    vg = v.reshape(b, rows, GRID_W, h, d)[:, row_idx]
    dr = row_idx - r[:, None] + (NA_ROWS - 1)
    dc = jnp.clip(col[None, :] - col[:, None], 1 - NA_COLS, NA_COLS - 1) + (NA_COLS - 1)
    bias = jnp.take(rpb[:, dr], dc, axis=-1).transpose(0, 1, 3, 2, 4).astype(F32)
    s_win = jnp.einsum('brqhd,brikhd->bhrqik', qg, kg).astype(F32) * scale + bias[None]
    s_win = jnp.where(col_mask[:, None, :], s_win, -jnp.inf).reshape(b, h, rows, GRID_W, kr * GRID_W)
    s_ctx = jnp.einsum('brqhd,bmhd->bhrqm', qg, k_ctx).astype(F32) * scale
    p = jax.nn.softmax(jnp.concatenate([s_win, s_ctx], axis=-1), axis=-1).astype(v.dtype)
    p_win = p[..., :kr * GRID_W].reshape(b, h, rows, GRID_W, kr, GRID_W)
    p_ctx = p[..., kr * GRID_W:]
    out = jnp.einsum('bhrqik,brikhd->brqhd', p_win, vg) + jnp.einsum('bhrqm,bmhd->brqhd', p_ctx, v_ctx)
    return out.reshape(b, n, h * d)


def _ssd_chunked(xs, dt, a, bm, cm, state0):
    b, l, h, p = xs.shape
    g, n = bm.shape[-2:]
    k = h // g
    nc = l // SSD_CHUNK
    xd = (xs.astype(F32) * dt[..., None]).reshape(b, nc, SSD_CHUNK, g, k, p)
    la = jnp.cumsum((dt * a).reshape(b, nc, SSD_CHUNK, g, k), axis=2)
    bc = bm.reshape(b, nc, SSD_CHUNK, g, n)
    cc = cm.reshape(b, nc, SSD_CHUNK, g, n)
    lower = jnp.tril(jnp.ones((SSD_CHUNK, SSD_CHUNK), dtype=bool))
    seg = la[:, :, :, None] - la[:, :, None, :]
    decay = jnp.exp(jnp.where(lower[:, :, None, None], seg, -jnp.inf))
    cb = jnp.einsum('bclgn,bcsgn->bclsg', cc, bc).astype(F32)
    y_diag = jnp.einsum('bclsgk,bcsgkp->bclgkp', cb[..., None] * decay, xd)
    decay_end = jnp.exp(la[:, :, -1:] - la)
    chunk_states = jnp.einsum('bclgn,bclgk,bclgkp->bcgkpn', bc.astype(F32), decay_end, xd)
    chunk_decay = jnp.exp(la[:, :, -1])

    def step(state, inp):
        dec, st = inp
        return state * dec[..., None, None] + st, state

    final, prev = lax.scan(step, state0, (chunk_decay.swapaxes(0, 1), chunk_states.swapaxes(0, 1)))
    y_off = jnp.einsum('bclgn,bcgkpn,bclgk->bclgkp', cc.astype(F32), prev.swapaxes(0, 1), jnp.exp(la))
    return (y_diag + y_off).reshape(b, l, h, p).astype(xs.dtype), final


def _bi_ssd(xs, bm, cm, dt_raw, dt_bias, a_log, d_skip, init_f, init_b):
    inits = (init_f, init_b)
    ys, finals = [], []
    for d in range(2):
        rev = (lambda t: jnp.flip(t, axis=1)) if d == 1 else (lambda t: t)
        dt = jax.nn.softplus((dt_raw[:, :, d] + dt_bias[d]).astype(F32))
        a = -jnp.exp(a_log[d].astype(F32))
        y, fin = _ssd_chunked(rev(xs), rev(dt), a, rev(bm), rev(cm), inits[d])
        ys.append(rev(y) + d_skip[d][:, None].astype(xs.dtype) * xs)
        finals.append(fin)
    return ys[0] + ys[1], finals[0], finals[1]


def _dwconv(u, w, bias):
    out = lax.conv_general_dilated(u, w[:, None, :], window_strides=(1,),
                                   padding=[(CONV_K // 2, CONV_K // 2)],
                                   dimension_numbers=('NWC', 'WIO', 'NWC'),
                                   feature_group_count=u.shape[-1])
    return out + bias


def _ab_inputs(h, w_in, conv_w, conv_b):
    b, l, _ = h.shape
    proj = h @ w_in
    i0 = SSM_INNER
    i1 = i0 + CONV_DIM
    i2 = i1 + 2 * SSM_HEADS
    z = proj[..., :i0]
    xbc = jax.nn.silu(_dwconv(proj[..., i0:i1], conv_w, conv_b))
    dt_raw = proj[..., i1:i2].reshape(b, l, 2, SSM_HEADS)
    gn = SSM_GROUPS * SSM_STATE
    xs = xbc[..., :SSM_INNER].reshape(b, l, SSM_HEADS, SSM_HEAD_DIM)
    bm = xbc[..., SSM_INNER:SSM_INNER + gn].reshape(b, l, SSM_GROUPS, SSM_STATE)
    cm = xbc[..., SSM_INNER + gn:].reshape(b, l, SSM_GROUPS, SSM_STATE)
    q, k, v = jnp.split(proj[..., i2:], 3, axis=-1)
    shp = (b, l, NA_HEADS, HEAD_DIM)
    return z, xs, bm, cm, dt_raw, q.reshape(shp), k.reshape(shp), v.reshape(shp)


def _gated_norm(y, z, w):
    b, l = z.shape[:2]
    u = (y.reshape(b, l, SSM_INNER) * jax.nn.silu(z)).astype(F32).reshape(b, l, SSM_NORM_GROUPS, -1)
    u = u * lax.rsqrt(jnp.mean(u * u, axis=-1, keepdims=True) + RMS_EPS)
    return (u.reshape(b, l, SSM_INNER) * w.astype(F32)).astype(z.dtype)


def _mixer_ab(h_lat, h_ctx, rows, need_ctx, w_in, conv_w, conv_b, dt_bias, a_log, d_skip,
              norm_w, q_norm, k_norm, rpb, w_out):
    zc, xc, bc, cc, dtc, qc, kc, vc = _ab_inputs(h_ctx, w_in, conv_w, conv_b)
    zl, xl, bl, cl, dtl, ql, kl, vl = _ab_inputs(h_lat, w_in, conv_w, conv_b)
    b = h_lat.shape[0]
    zero = jnp.zeros((b, SSM_GROUPS, SSM_HEADS // SSM_GROUPS, SSM_HEAD_DIM, SSM_STATE), F32)
    yc, sf, sb = _bi_ssd(xc, bc, cc, dtc, dt_bias, a_log, d_skip, zero, zero)
    yl, _, _ = _bi_ssd(xl, bl, cl, dtl, dt_bias, a_log, d_skip, sf, sb)
    kc = _rms(kc, k_norm)
    al = _neighbourhood_attention(_rms(ql, q_norm), _rms(kl, k_norm), vl, kc, vc, rpb, rows)
    out_lat = jnp.concatenate([_gated_norm(yl, zl, norm_w), al], axis=-1) @ w_out
    out_ctx = None
    if need_ctx:
        ac = _attend_blocks(_rms(qc, q_norm), kc, vc)
        out_ctx = jnp.concatenate([_gated_norm(yc, zc, norm_w), ac], axis=-1) @ w_out
    return out_lat, out_ctx


def _gqa_inputs(h, w_qkv, q_norm, k_norm):
    b, l, _ = h.shape
    p = h @ w_qkv
    qd = ATTN_HEADS * HEAD_DIM
    kd = KV_HEADS * HEAD_DIM
    q = _rms(p[..., :qd].reshape(b, l, ATTN_HEADS, HEAD_DIM), q_norm)
    k = _rms(p[..., qd:qd + kd].reshape(b, l, KV_HEADS, HEAD_DIM), k_norm)
    v = p[..., qd + kd:].reshape(b, l, KV_HEADS, HEAD_DIM)
    return q, k, v


def _mixer_c(h_lat, h_ctx, cos, sin, need_ctx, w_qkv, q_norm, k_norm, w_out):
    ql, kl, vl = _gqa_inputs(h_lat, w_qkv, q_norm, k_norm)
    qc, kc, vc = _gqa_inputs(h_ctx, w_qkv, q_norm, k_norm)
    ql = _rope(ql, cos, sin)
    kl = _rope(kl, cos, sin)
    k_all = jnp.concatenate([kc, kl], axis=1)
    v_all = jnp.concatenate([vc, vl], axis=1)
    out_lat = _attend_blocks(ql, k_all, v_all) @ w_out
    out_ctx = _attend_blocks(qc, kc, vc) @ w_out if need_ctx else None
    return out_lat, out_ctx


def setup_inputs(seed: int = 0) -> dict:
    key = jax.random.key(seed)
    ks = iter(jax.random.split(key, 40))
    n_even = (DEPTH + 1) // 2
    n_odd = DEPTH // 2

    def nrm(shape, scale):
        return jax.random.normal(next(ks), shape, jnp.float32) * scale

    x = nrm((BATCH, SEQ, D_MODEL), 1.0)
    c = nrm((BATCH, D_MODEL), 1.0)
    ctx = nrm((BATCH, CTX_LEN, D_MODEL), 1.0)
    c_ctx = nrm((D_MODEL,), 1.0)
    w_mod = nrm((DEPTH, D_MODEL, N_MOD * D_MODEL), 0.5 * D_MODEL ** -0.5)
    b_mod = nrm((DEPTH, N_MOD * D_MODEL), 0.02)
    norm1_w = 1.0 + nrm((DEPTH, D_MODEL), 0.05)
    norm2_w = 1.0 + nrm((DEPTH, D_MODEL), 0.05)
    w_mlp_in = nrm((DEPTH, D_MODEL, MLP_HIDDEN), D_MODEL ** -0.5)
    w_mlp_out = nrm((DEPTH, MLP_HIDDEN, D_MODEL), MLP_HIDDEN ** -0.5)
    ab_w_in = nrm((n_even, D_MODEL, AB_IN), D_MODEL ** -0.5)
    ab_conv_w = nrm((n_even, CONV_K, CONV_DIM), CONV_K ** -0.5)
    ab_conv_b = nrm((n_even, CONV_DIM), 0.02)
    dt0 = jnp.exp(jax.random.uniform(next(ks), (n_even, 2, SSM_HEADS), jnp.float32,
                                     minval=math.log(1e-3), maxval=math.log(1e-1)))
    ab_dt_bias = dt0 + jnp.log(-jnp.expm1(-dt0))
    ab_a_log = jnp.log(jax.random.uniform(next(ks), (n_even, 2, SSM_HEADS), jnp.float32, minval=1.0, maxval=16.0))
    ab_d_skip = 1.0 + nrm((n_even, 2, SSM_HEADS), 0.1)
    ab_norm_w = 1.0 + nrm((n_even, SSM_INNER), 0.05)
    ab_q_norm = 1.0 + nrm((n_even, HEAD_DIM), 0.05)
    ab_k_norm = 1.0 + nrm((n_even, HEAD_DIM), 0.05)
    ab_rpb = nrm((n_even, NA_HEADS, 2 * NA_ROWS - 1, 2 * NA_COLS - 1), 0.1)
    ab_w_out = nrm((n_even, AB_OUT, D_MODEL), AB_OUT ** -0.5)
    c_w_qkv = nrm((n_odd, D_MODEL, C_QKV), D_MODEL ** -0.5)
    c_q_norm = 1.0 + nrm((n_odd, HEAD_DIM), 0.05)
    c_k_norm = 1.0 + nrm((n_odd, HEAD_DIM), 0.05)
    c_w_out = nrm((n_odd, ATTN_HEADS * HEAD_DIM, D_MODEL), (ATTN_HEADS * HEAD_DIM) ** -0.5)
    return {'x': x, 'c': c, 'ctx': ctx, 'c_ctx': c_ctx, 'w_mod': w_mod, 'b_mod': b_mod,
            'norm1_w': norm1_w, 'norm2_w': norm2_w, 'w_mlp_in': w_mlp_in, 'w_mlp_out': w_mlp_out,
            'ab_w_in': ab_w_in, 'ab_conv_w': ab_conv_w, 'ab_conv_b': ab_conv_b, 'ab_dt_bias': ab_dt_bias,
            'ab_a_log': ab_a_log, 'ab_d_skip': ab_d_skip, 'ab_norm_w': ab_norm_w, 'ab_q_norm': ab_q_norm,
            'ab_k_norm': ab_k_norm, 'ab_rpb': ab_rpb, 'ab_w_out': ab_w_out, 'c_w_qkv': c_w_qkv,
            'c_q_norm': c_q_norm, 'c_k_norm': c_k_norm, 'c_w_out': c_w_out}


def reference(x, c, ctx, c_ctx, w_mod, b_mod, norm1_w, norm2_w, w_mlp_in, w_mlp_out,
              ab_w_in, ab_conv_w, ab_conv_b, ab_dt_bias, ab_a_log, ab_d_skip, ab_norm_w,
              ab_q_norm, ab_k_norm, ab_rpb, ab_w_out, c_w_qkv, c_q_norm, c_k_norm, c_w_out):
    n_lat = x.shape[1]
    rows = n_lat // GRID_W
    cos, sin = _rope_tables(n_lat, x.dtype)
    s_lat = jax.nn.silu(c)
    s_ctx = jax.nn.silu(c_ctx)
    for layer in range(DEPTH):
        need_ctx = layer < DEPTH - 1
        mod_l = (s_lat @ w_mod[layer] + b_mod[layer])[:, None, :]
        sh1, sc1, g1, sh2, sc2, g2 = jnp.split(mod_l, N_MOD, axis=-1)
        mod_c = s_ctx @ w_mod[layer] + b_mod[layer]
        csh1, csc1, cg1, csh2, csc2, cg2 = jnp.split(mod_c, N_MOD, axis=-1)
        h_lat = _rms(x, norm1_w[layer]) * (1 + sc1) + sh1
        h_ctx = _rms(ctx, norm1_w[layer]) * (1 + csc1) + csh1
        i = layer // 2
        if layer % 2 == 0:
            m_lat, m_ctx = _mixer_ab(h_lat, h_ctx, rows, need_ctx, ab_w_in[i], ab_conv_w[i], ab_conv_b[i],
                                     ab_dt_bias[i], ab_a_log[i], ab_d_skip[i], ab_norm_w[i],
                                     ab_q_norm[i], ab_k_norm[i], ab_rpb[i], ab_w_out[i])
        else:
            m_lat, m_ctx = _mixer_c(h_lat, h_ctx, cos, sin, need_ctx, c_w_qkv[i], c_q_norm[i],
                                    c_k_norm[i], c_w_out[i])
        x = x + g1 * m_lat
        x = x + g2 * _mlp(_rms(x, norm2_w[layer]) * (1 + sc2) + sh2, w_mlp_in[layer], w_mlp_out[layer])
        if need_ctx:
            ctx = ctx + cg1 * m_ctx
            ctx = ctx + cg2 * _mlp(_rms(ctx, norm2_w[layer]) * (1 + csc2) + csh2, w_mlp_in[layer], w_mlp_out[layer])
    return x
```

```python
import functools
import math

import numpy as np
import jax
import jax.numpy as jnp
from jax import lax
from jax.experimental import pallas as pl
from jax.experimental.pallas import tpu as pltpu

F32 = jnp.float32
BF16 = jnp.bfloat16

GRID_W = 64
HEAD_DIM = 128
ROPE_THETA = 10000.0
RMS_EPS = 1e-6
N_MOD = 6
SSM_HEAD_DIM = 64
SSM_GROUPS = 8
SSM_STATE = 128
SSM_NORM_GROUPS = 8
CONV_K = 5
NA_ROWS = 8
NA_COLS = 16

LANES = 128
SUBLANES = 8
VMEM_LIMIT_BYTES = 48 * 1024 * 1024

ROW_TILE = 256
SSD_CHUNK = 256
NA_QROWS = 4
NA_WROWS = 12


def _pick(n, candidates):
    for c in candidates:
        if n % c == 0:
            return c
    raise ValueError(f"no tile in {candidates} divides {n}")


def _cparams(sem):
    return pltpu.CompilerParams(dimension_semantics=sem, vmem_limit_bytes=VMEM_LIMIT_BYTES)


def _sigmoid(v):
    return 1.0 / (1.0 + jnp.exp(-v))


def _softplus(v):
    return jnp.maximum(v, 0.0) + jnp.log(1.0 + jnp.exp(-jnp.abs(v)))


def _dot(a, b):
    return jnp.dot(a, b, preferred_element_type=F32)


def _dot_nt(a, b):
    return lax.dot_general(a, b, (((1,), (1,)), ((), ())), preferred_element_type=F32)


def _dot_exact(a, b):
    return jnp.dot(a, b, preferred_element_type=F32, precision=lax.Precision.HIGHEST)


def _mod_kernel(v_ref, w_ref, b_ref, o_ref):
    v = v_ref[...]
    s = (v * _sigmoid(v)).astype(BF16)
    o_ref[0] = _dot(s, w_ref[0].astype(BF16)) + b_ref[0]


def _modulation(vecs, w_mod, b_mod):
    depth, d, n = w_mod.shape
    tn = _pick(n, (1024, 512, 256, 128))
    return pl.pallas_call(
        _mod_kernel,
        grid=(depth, n // tn),
        in_specs=[pl.BlockSpec((SUBLANES, d), lambda l, j: (0, 0)),
                  pl.BlockSpec((1, d, tn), lambda l, j: (l, 0, j)),
                  pl.BlockSpec((1, 1, tn), lambda l, j: (l, 0, j))],
        out_specs=pl.BlockSpec((1, SUBLANES, tn), lambda l, j: (l, 0, j)),
        out_shape=jax.ShapeDtypeStruct((depth, SUBLANES, n), F32),
        compiler_params=_cparams(("arbitrary", "arbitrary")),
        name="modulation",
    )(vecs, w_mod, b_mod.reshape(depth, 1, n))


def _normmod_kernel(x_ref, w_ref, m_ref, o_ref):
    x = x_ref[...]
    y = x * lax.rsqrt(jnp.mean(x * x, axis=-1, keepdims=True) + RMS_EPS)
    m = m_ref[0]
    o_ref[...] = ((y * w_ref[...]) * (1.0 + m[0:1, :]) + m[1:2, :]).astype(o_ref.dtype)


def _normmod(x, w, scsh, n_lat):
    r, d = x.shape
    nlb = n_lat // ROW_TILE
    return pl.pallas_call(
        _normmod_kernel,
        grid=(r // ROW_TILE,),
        in_specs=[pl.BlockSpec((ROW_TILE, d), lambda i: (i, 0)),
                  pl.BlockSpec((1, d), lambda i: (0, 0)),
                  pl.BlockSpec((1, 2, d), lambda i: (i // nlb, 0, 0))],
        out_specs=pl.BlockSpec((ROW_TILE, d), lambda i: (i, 0)),
        out_shape=jax.ShapeDtypeStruct((r, d), BF16),
        compiler_params=_cparams(("arbitrary",)),
        name="normmod",
    )(x, w.reshape(1, d), scsh)


def _mm_kernel(*refs, nk, epilogue, n_lat, tm):
    if epilogue == "resid":
        x_ref, w_ref, res_ref, gate_ref, o_ref = refs[:5]
        scratch = refs[5:]
    else:
        x_ref, w_ref, o_ref = refs[:3]
        scratch = refs[3:]

    def finish(acc):
        if epilogue == "relu2":
            a = jnp.maximum(acc, 0.0)
            o_ref[...] = (a * a).astype(o_ref.dtype)
        elif epilogue == "resid":
            row = pl.program_id(0) * tm + lax.broadcasted_iota(jnp.int32, (tm, 1), 0)
            g = jnp.where(row < n_lat, gate_ref[0:1, :], gate_ref[1:2, :])
            o_ref[...] = res_ref[...] + g * acc
        else:
            o_ref[...] = acc.astype(o_ref.dtype)

    if nk == 1:
        finish(_dot(x_ref[...], w_ref[...]))
    else:
        acc_ref, = scratch
        k = pl.program_id(2)

        @pl.when(k == 0)
        def _():
            acc_ref[...] = jnp.zeros_like(acc_ref)

        acc_ref[...] += _dot(x_ref[...], w_ref[...])

        @pl.when(k == nk - 1)
        def _():
            finish(acc_ref[...])


def _matmul(x, w, *, rows=None, out_dtype=F32, epilogue="none", res=None, gate=None, n_lat=0, name="matmul"):
    r = x.shape[0] if rows is None else rows
    kdim, n = w.shape
    tm = _pick(r, (1024, 768, 512, 256))
    tn = _pick(n, (1024, 512, 256, 128))
    tk = kdim if kdim <= 2048 else _pick(kdim, (2048, 1024, 512))
    nk = kdim // tk
    in_specs = [pl.BlockSpec((tm, tk), lambda i, j, k: (i, k)),
                pl.BlockSpec((tk, tn), lambda i, j, k: (k, j))]
    args = [x, w]
    if epilogue == "resid":
        in_specs += [pl.BlockSpec((tm, tn), lambda i, j, k: (i, j)),
                     pl.BlockSpec((2, tn), lambda i, j, k: (0, j))]
        args += [res, gate]
    return pl.pallas_call(
        functools.partial(_mm_kernel, nk=nk, epilogue=epilogue, n_lat=n_lat, tm=tm),
        grid=(r // tm, n // tn, nk),
        in_specs=in_specs,
        out_specs=pl.BlockSpec((tm, tn), lambda i, j, k: (i, j)),
        out_shape=jax.ShapeDtypeStruct((r, n), out_dtype),
        scratch_shapes=[] if nk == 1 else [pltpu.VMEM((tm, tn), F32)],
        compiler_params=_cparams(("arbitrary", "arbitrary", "arbitrary")),
        name=name,
    )(*args)


def _prep_kernel(*refs, n_norm, n_rope, n_blocks):
    if n_rope:
        p_ref, w_ref, cos_ref, sin_ref, o_ref = refs
        cos = cos_ref[...]
        sin = sin_ref[...]
        even = (lax.broadcasted_iota(jnp.int32, cos.shape, 1) % 2) == 0
    else:
        p_ref, w_ref, o_ref = refs
    for b in range(n_blocks):
        sl = slice(b * HEAD_DIM, (b + 1) * HEAD_DIM)
        u = p_ref[:, sl]
        if b < n_norm:
            u = u * lax.rsqrt(jnp.mean(u * u, axis=-1, keepdims=True) + RMS_EPS) * w_ref[:, sl]
        if b < n_rope:
            partner = jnp.where(even, pltpu.roll(u, HEAD_DIM - 1, 1), pltpu.roll(u, 1, 1))
            u = u * cos + partner * sin
        o_ref[:, sl] = u.astype(o_ref.dtype)


def _prep(p, wvec, n_norm, n_rope=0, cos=None, sin=None):
    r, n = p.shape
    in_specs = [pl.BlockSpec((ROW_TILE, n), lambda i: (i, 0)),
                pl.BlockSpec((1, n), lambda i: (0, 0))]
    args = [p, wvec.reshape(1, n)]
    if n_rope:
        in_specs += [pl.BlockSpec((ROW_TILE, HEAD_DIM), lambda i: (i, 0))] * 2
        args += [cos, sin]
    return pl.pallas_call(
        functools.partial(_prep_kernel, n_norm=n_norm, n_rope=n_rope, n_blocks=n // HEAD_DIM),
        grid=(r // ROW_TILE,),
        in_specs=in_specs,
        out_specs=pl.BlockSpec((ROW_TILE, n), lambda i: (i, 0)),
        out_shape=jax.ShapeDtypeStruct((r, n), BF16),
        compiler_params=_cparams(("arbitrary",)),
        name="qkv_prep",
    )(*args)


def _attn_kernel(q_ref, k_ref, v_ref, o_ref, qs_ref, m_ref, l_ref, acc_ref, *, group, tq, tk, n_lat, n_all):
    nqb_lat = n_lat // tq
    for g in range(group):
        qs_ref[g * tq:(g + 1) * tq, :] = q_ref[:, g * HEAD_DIM:(g + 1) * HEAD_DIM]

    def attend(first, n_chunks, chunk):
        m_ref[...] = jnp.full(m_ref.shape, -jnp.inf, F32)
        l_ref[...] = jnp.zeros(l_ref.shape, F32)
        acc_ref[...] = jnp.zeros(acc_ref.shape, F32)

        def body(c, carry):
            start = pl.multiple_of(first + c * chunk, chunk)
            kc = k_ref[pl.ds(start, chunk), :]
            vc = v_ref[pl.ds(start, chunk), :]
            s = _dot_nt(qs_ref[...], kc)
            m_prev = m_ref[...]
            m_new = jnp.maximum(m_prev, jnp.max(s, axis=1, keepdims=True))
            alpha = jnp.exp(m_prev - m_new)
            p = jnp.exp(s - pltpu.repeat(m_new, chunk // LANES, 1))
            l_ref[...] = alpha * l_ref[...] + jnp.sum(p, axis=1, keepdims=True)
            acc_ref[...] = alpha * acc_ref[...] + _dot(p.astype(BF16), vc)
            m_ref[...] = m_new
            return carry

        lax.fori_loop(0, n_chunks, body, 0)
        out = acc_ref[...] / l_ref[...]
        for g in range(group):
            o_ref[:, g * HEAD_DIM:(g + 1) * HEAD_DIM] = out[g * tq:(g + 1) * tq, :].astype(o_ref.dtype)

    qi = pl.program_id(1)

    @pl.when(qi < nqb_lat)
    def _():
        attend(0, n_all // tk, tk)

    @pl.when(qi >= nqb_lat)
    def _():
        attend(n_lat, (n_all - n_lat) // tq, tq)


def _attention(qkv, *, n_heads, n_kv, n_lat, q_rows):
    n_all = qkv.shape[0]
    group = n_heads // n_kv
    tq = ROW_TILE
    tk = _pick(n_all, (768, 512, 256))
    gw = group * HEAD_DIM
    kernel = functools.partial(_attn_kernel, group=group, tq=tq, tk=tk, n_lat=n_lat, n_all=n_all)
    return pl.pallas_call(
        kernel,
        grid=(n_kv, q_rows // tq),
        in_specs=[pl.BlockSpec((tq, gw), lambda h, i: (i, h)),
                  pl.BlockSpec((n_all, HEAD_DIM), lambda h, i: (0, n_heads + h)),
                  pl.BlockSpec((n_all, HEAD_DIM), lambda h, i: (0, n_heads + n_kv + h))],
        out_specs=pl.BlockSpec((tq, gw), lambda h, i: (i, h)),
        out_shape=jax.ShapeDtypeStruct((q_rows, n_heads * HEAD_DIM), BF16),
        scratch_shapes=[pltpu.VMEM((group * tq, HEAD_DIM), BF16),
                        pltpu.VMEM((group * tq, LANES), F32),
                        pltpu.VMEM((group * tq, LANES), F32),
                        pltpu.VMEM((group * tq, HEAD_DIM), F32)],
        compiler_params=_cparams(("arbitrary", "arbitrary")),
        name="gqa_attention",
    )(qkv, qkv, qkv)


def _na_kernel(q_ref, k_ref, v_ref, bias_ref, o_ref, *, n_lat, rows, n_ctx):
    nb = rows // NA_QROWS
    b = pl.program_id(1)
    q = q_ref[...]
    kctx = k_ref[n_lat:n_lat + n_ctx, :]
    vctx = v_ref[n_lat:n_lat + n_ctx, :]
    s_c = _dot_nt(q, kctx)

    @pl.when(b < nb)
    def _():
        w0 = jnp.clip(NA_QROWS * b - NA_ROWS // 2, 0, rows - NA_WROWS)
        start = pl.multiple_of(w0 * GRID_W, GRID_W)
        kw = k_ref[pl.ds(start, NA_WROWS * GRID_W), :]
        vw = v_ref[pl.ds(start, NA_WROWS * GRID_W), :]
        s_w = _dot_nt(q, kw) + bias_ref[0, 0]
        m = jnp.maximum(jnp.max(s_w, axis=1, keepdims=True), jnp.max(s_c, axis=1, keepdims=True))
        p_w = jnp.exp(s_w - m)
        p_c = jnp.exp(s_c - m)
        l = jnp.sum(p_w, axis=1, keepdims=True) + jnp.sum(p_c, axis=1, keepdims=True)
        o = _dot(p_w.astype(BF16), vw) + _dot(p_c.astype(BF16), vctx)
        o_ref[...] = (o / l).astype(o_ref.dtype)

    @pl.when(b >= nb)
    def _():
        m = jnp.max(s_c, axis=1, keepdims=True)
        p_c = jnp.exp(s_c - m)
        l = jnp.sum(p_c, axis=1, keepdims=True)
        o_ref[...] = (_dot(p_c.astype(BF16), vctx) / l).astype(o_ref.dtype)


def _na_bias_tiles(rpb, rows):
    h = rpb.shape[0]
    col = np.arange(GRID_W)
    c0 = np.clip(col - NA_COLS // 2, 0, GRID_W - NA_COLS)
    col_ok = (col[None, :] >= c0[:, None]) & (col[None, :] < c0[:, None] + NA_COLS)
    dc = np.clip(col[None, :] - col[:, None], 1 - NA_COLS, NA_COLS - 1) + (NA_COLS - 1)
    toep = jnp.where(col_ok[None, None], rpb[:, :, dc], -jnp.inf)
    neg = jnp.full((h, GRID_W, GRID_W), -jnp.inf, F32)
    half = NA_ROWS // 2
    kinds = (lambda j: (j, 0), lambda j: (half + j, j), lambda j: (NA_ROWS + j, half))
    tiles = []
    for kind in kinds:
        qrows = []
        for j in range(NA_QROWS):
            q_row, win = kind(j)
            blocks = []
            for ik in range(NA_WROWS):
                dr = ik - q_row + (NA_ROWS - 1)
                ok = win <= ik < win + NA_ROWS
                blocks.append(toep[:, dr] if ok else neg)
            qrows.append(jnp.concatenate(blocks, axis=-1))
        tiles.append(jnp.concatenate(qrows, axis=-2))
    return jnp.stack(tiles, axis=0)


def _neighbourhood_attention(qkv, bias, *, n_heads, n_lat, n_ctx):
    r = qkv.shape[0]
    rows = n_lat // GRID_W
    nb = rows // NA_QROWS
    tq = NA_QROWS * GRID_W
    assert tq == n_ctx and rows >= NA_WROWS + NA_QROWS

    def bias_map(h, b):
        return (jnp.where(b == 0, 0, jnp.where(b >= nb - 1, 2, 1)), h, 0, 0)

    kernel = functools.partial(_na_kernel, n_lat=n_lat, rows=rows, n_ctx=n_ctx)
    return pl.pallas_call(
        kernel,
        grid=(n_heads, r // tq),
        in_specs=[pl.BlockSpec((tq, HEAD_DIM), lambda h, b: (b, h)),
                  pl.BlockSpec((r, HEAD_DIM), lambda h, b: (0, n_heads + h)),
                  pl.BlockSpec((r, HEAD_DIM), lambda h, b: (0, 2 * n_heads + h)),
                  pl.BlockSpec((1, 1, tq, NA_WROWS * GRID_W), bias_map)],
        out_specs=pl.BlockSpec((tq, HEAD_DIM), lambda h, b: (b, h)),
        out_shape=jax.ShapeDtypeStruct((r, n_heads * HEAD_DIM), BF16),
        compiler_params=_cparams(("arbitrary", "arbitrary")),
        name="neighbourhood_attention",
    )(qkv, qkv, qkv, bias)


def _conv_kernel(u_ref, prev_ref, next_ref, w_ref, b_ref, o_ref, ext_ref, *, n_lat):
    tr = u_ref.shape[0]
    i = pl.program_id(0)
    nlb = n_lat // tr
    has_prev = jnp.logical_and(i != 0, i != nlb)
    has_next = i < nlb - 1
    ext_ref[0:SUBLANES, :] = jnp.where(has_prev, prev_ref[...], 0.0)
    ext_ref[SUBLANES:SUBLANES + tr, :] = u_ref[...]
    ext_ref[SUBLANES + tr:, :] = jnp.where(has_next, next_ref[...], 0.0)
    acc = jnp.broadcast_to(b_ref[...], o_ref.shape)
    for j in range(CONV_K):
        off = SUBLANES - CONV_K // 2 + j
        acc = acc + w_ref[j:j + 1, :] * ext_ref[off:off + tr, :]
    o_ref[...] = acc * _sigmoid(acc)


def _conv_silu(proj, col0, conv_w, conv_b, n_lat):
    r = proj.shape[0]
    ch = conv_w.shape[1]
    tr = ROW_TILE
    tc = _pick(ch, (1024, 512, 256, 128))
    cb0 = col0 // tc
    assert col0 % tc == 0
    rb = tr // SUBLANES
    last = r // SUBLANES - 1
    return pl.pallas_call(
        functools.partial(_conv_kernel, n_lat=n_lat),
        grid=(r // tr, ch // tc),
        in_specs=[pl.BlockSpec((tr, tc), lambda i, j: (i, cb0 + j)),
                  pl.BlockSpec((SUBLANES, tc), lambda i, j: (jnp.maximum(i * rb - 1, 0), cb0 + j)),
                  pl.BlockSpec((SUBLANES, tc), lambda i, j: (jnp.minimum((i + 1) * rb, last), cb0 + j)),
                  pl.BlockSpec((CONV_K, tc), lambda i, j: (0, j)),
                  pl.BlockSpec((1, tc), lambda i, j: (0, j))],
        out_specs=pl.BlockSpec((tr, tc), lambda i, j: (i, j)),
        out_shape=jax.ShapeDtypeStruct((r, ch), F32),
        scratch_shapes=[pltpu.VMEM((tr + 2 * SUBLANES, tc), F32)],
        compiler_params=_cparams(("arbitrary", "arbitrary")),
        name="conv_silu",
    )(proj, proj, proj, conv_w, conv_b.reshape(1, ch))


def _ssd_kernel(xs_ref, b_ref, c_ref, dt_ref, dtt_ref, bias_ref, biast_ref, alog_ref, alogt_ref, e_ref,
                y_ref, st_ref, *, n_groups, heads_per_group):
    t = xs_ref.shape[0]
    d = pl.program_id(0)
    c = pl.program_id(1)
    gw = heads_per_group * SSM_HEAD_DIM

    @pl.when(c == 0)
    def _():
        st_ref[...] = jnp.zeros(st_ref.shape, F32)

    dt = _softplus(dt_ref[0] + bias_ref[0])
    dtt = _softplus(dtt_ref[0] + biast_ref[0])
    da = dt * (-jnp.exp(alog_ref[0]))
    dat = dtt * (-jnp.exp(alogt_ref[0]))
    row = lax.broadcasted_iota(jnp.int32, (t, t), 0)
    col = lax.broadcasted_iota(jnp.int32, (t, t), 1)
    lag = jnp.where(d == 0, row - col, col - row)
    valid = lag >= 0
    valid_t = lag <= 0
    cum = _dot_exact(valid.astype(F32), da)
    cumt = _dot_exact(dat, valid_t.astype(F32))
    tot = jnp.sum(da, axis=0, keepdims=True)
    stack = jnp.concatenate([dt, jnp.exp(cum), jnp.exp(tot - cum),
                             jnp.broadcast_to(jnp.exp(tot), (SUBLANES, tot.shape[1]))], axis=0)
    wide = _dot_exact(stack, e_ref[...])
    dt_w = wide[0:t]
    ecum_w = wide[t:2 * t]
    dend_w = wide[2 * t:3 * t]
    cdec_w = wide[3 * t:3 * t + 1]
    xd = xs_ref[...] * dt_w
    xdw = xd * dend_w
    lane_head = lax.broadcasted_iota(jnp.int32, (t, gw), 1) // SSM_HEAD_DIM

    for g in range(n_groups):
        ns = slice(g * SSM_STATE, (g + 1) * SSM_STATE)
        hs = slice(g * gw, (g + 1) * gw)
        bg = b_ref[:, ns]
        cg = c_ref[:, ns].astype(BF16)
        cb = _dot_nt(cg, bg.astype(BF16))
        acc = _dot(cg, st_ref[g].astype(BF16)) * ecum_w[:, hs]
        xd_g = xd[:, hs]
        for k in range(heads_per_group):
            h = g * heads_per_group + k
            seg = cum[:, h:h + 1] - cumt[h:h + 1, :]
            decay = jnp.exp(jnp.where(valid, seg, -jnp.inf))
            mh = (cb * decay).astype(BF16)
            xk = jnp.where(lane_head == k, xd_g, 0.0).astype(BF16)
            acc = acc + _dot(mh, xk)
        y_ref[0, :, hs] = acc
        st_ref[g] = st_ref[g] * cdec_w[:, hs] + _dot(bg.T.astype(BF16), xdw[:, hs].astype(BF16))


def _ssd(xbc, dt_raw, dt_bias, a_log, n_lat, n_inner):
    r = xbc.shape[0]
    t = SSD_CHUNK
    heads = n_inner // SSM_HEAD_DIM
    gn = SSM_GROUPS * SSM_STATE
    hpg = heads // SSM_GROUPS
    n_lc = n_lat // t
    assert r - n_lat == t and n_inner % gn == 0
    dt3 = dt_raw[:, :2 * heads].reshape(r, 2, heads).transpose(1, 0, 2)
    dt3t = dt3.transpose(0, 2, 1)
    expand = jnp.asarray(np.repeat(np.eye(heads, dtype=np.float32), SSM_HEAD_DIM, axis=1))

    def chunk(d, c):
        return jnp.where(c == 0, n_lc, jnp.where(d == 0, c - 1, n_lc - c))

    xb = n_inner // gn
    kernel = functools.partial(_ssd_kernel, n_groups=SSM_GROUPS, heads_per_group=hpg)
    return pl.pallas_call(
        kernel,
        grid=(2, n_lc + 1),
        in_specs=[pl.BlockSpec((t, n_inner), lambda d, c: (chunk(d, c), 0)),
                  pl.BlockSpec((t, gn), lambda d, c: (chunk(d, c), xb)),
                  pl.BlockSpec((t, gn), lambda d, c: (chunk(d, c), xb + 1)),
                  pl.BlockSpec((1, t, heads), lambda d, c: (d, chunk(d, c), 0)),
                  pl.BlockSpec((1, heads, t), lambda d, c: (d, 0, chunk(d, c))),
                  pl.BlockSpec((1, 1, heads), lambda d, c: (d, 0, 0)),
                  pl.BlockSpec((1, heads, 1), lambda d, c: (d, 0, 0)),
                  pl.BlockSpec((1, 1, heads), lambda d, c: (d, 0, 0)),
                  pl.BlockSpec((1, heads, 1), lambda d, c: (d, 0, 0)),
                  pl.BlockSpec((heads, n_inner), lambda d, c: (0, 0))],
        out_specs=pl.BlockSpec((1, t, n_inner), lambda d, c: (d, chunk(d, c), 0)),
        out_shape=jax.ShapeDtypeStruct((2, r, n_inner), F32),
        scratch_shapes=[pltpu.VMEM((SSM_GROUPS, SSM_STATE, hpg * SSM_HEAD_DIM), F32)],
        compiler_params=_cparams(("arbitrary", "arbitrary")),
        name="ssd_scan",
    )(xbc, xbc, xbc, dt3, dt3t, dt_bias.reshape(2, 1, heads), dt_bias.reshape(2, heads, 1),
      a_log.reshape(2, 1, heads), a_log.reshape(2, heads, 1), expand)


def _gated_norm_kernel(y_ref, xs_ref, z_ref, dskip_ref, w_ref, o_ref, *, n_groups):
    z = z_ref[...]
    u = (y_ref[0] + y_ref[1] + dskip_ref[...] * xs_ref[...]) * (z * _sigmoid(z))
    gw = u.shape[1] // n_groups
    for g in range(n_groups):
        sl = slice(g * gw, (g + 1) * gw)
        ug = u[:, sl]
        ug = ug * lax.rsqrt(jnp.mean(ug * ug, axis=-1, keepdims=True) + RMS_EPS)
        o_ref[:, sl] = (ug * w_ref[:, sl]).astype(o_ref.dtype)


def _gated_norm(y, xbc, proj, dskip_w, norm_w):
    _, r, n = y.shape
    return pl.pallas_call(
        functools.partial(_gated_norm_kernel, n_groups=SSM_NORM_GROUPS),
        grid=(r // ROW_TILE,),
        in_specs=[pl.BlockSpec((2, ROW_TILE, n), lambda i: (0, i, 0)),
                  pl.BlockSpec((ROW_TILE, n), lambda i: (i, 0)),
                  pl.BlockSpec((ROW_TILE, n), lambda i: (i, 0)),
                  pl.BlockSpec((1, n), lambda i: (0, 0)),
                  pl.BlockSpec((1, n), lambda i: (0, 0))],
        out_specs=pl.BlockSpec((ROW_TILE, n), lambda i: (i, 0)),
        out_shape=jax.ShapeDtypeStruct((r, n), BF16),
        compiler_params=_cparams(("arbitrary",)),
        name="gated_norm",
    )(y, xbc, proj, dskip_w.reshape(1, n), norm_w.reshape(1, n))


def _rope_tables(n_lat, n_ctx):
    t = jnp.arange(n_lat)
    half = HEAD_DIM // 2
    inv = 1.0 / (ROPE_THETA ** (jnp.arange(0, half, 2, dtype=F32) / half))
    ang = jnp.concatenate([(t // GRID_W).astype(F32)[:, None] * inv,
                           (t % GRID_W).astype(F32)[:, None] * inv], axis=-1)
    cos = jnp.repeat(jnp.cos(ang), 2, axis=-1)
    sin = jnp.repeat(jnp.sin(ang), 2, axis=-1) * jnp.asarray(np.tile(np.array([-1.0, 1.0], np.float32), half))
    cos = jnp.concatenate([cos, jnp.ones((n_ctx, HEAD_DIM), F32)], axis=0)
    sin = jnp.concatenate([sin, jnp.zeros((n_ctx, HEAD_DIM), F32)], axis=0)
    return cos, sin


def kernel(x, c, ctx, c_ctx, w_mod, b_mod, norm1_w, norm2_w, w_mlp_in, w_mlp_out, ab_w_in, ab_conv_w, ab_conv_b,
           ab_dt_bias, ab_a_log, ab_d_skip, ab_norm_w, ab_q_norm, ab_k_norm, ab_rpb, ab_w_out, c_w_qkv, c_q_norm,
           c_k_norm, c_w_out):
    _, n_lat, d = x.shape
    n_ctx = ctx.shape[1]
    depth = w_mod.shape[0]
    assert x.shape[0] == 1 and n_ctx == ROW_TILE and n_lat % 1024 == 0
    n_inner = d
    ssm_heads = n_inner // SSM_HEAD_DIM
    conv_dim = ab_conv_w.shape[-1]
    na_heads = ab_rpb.shape[1]
    attn_heads = c_w_out.shape[1] // HEAD_DIM
    kv_heads = (c_w_qkv.shape[-1] // HEAD_DIM - attn_heads) // 2
    qk_scale = HEAD_DIM ** -0.5

    xr = jnp.concatenate([x[0], ctx[0]], axis=0)
    vecs = jnp.zeros((SUBLANES, d), F32).at[0].set(c[0]).at[1].set(c_ctx)
    mods = _modulation(vecs, w_mod, b_mod)[:, :2].reshape(depth, 2, N_MOD, d)
    cos, sin = _rope_tables(n_lat, n_ctx)

    for layer in range(depth):
        last = layer == depth - 1
        i = layer // 2
        mod = mods[layer]
        h = _normmod(xr, norm1_w[layer], mod[:, (1, 0), :], n_lat)
        if layer % 2 == 0:
            i0 = n_inner
            i1 = i0 + conv_dim
            i2 = i1 + 2 * ssm_heads
            w_in = ab_w_in[i]
            proj = _matmul(h, w_in[:, :i1].astype(BF16), name="ab_in_zxbc")
            w_dt = jnp.pad(w_in[:, i1:i2], ((0, 0), (0, LANES - (i2 - i1)))).astype(BF16)
            dt_raw = _matmul(h, w_dt, name="ab_in_dt")
            qkv = _matmul(h, w_in[:, i2:].astype(BF16), name="ab_in_qkv")
            xbc = _conv_silu(proj, i0, ab_conv_w[i], ab_conv_b[i], n_lat)
            y = _ssd(xbc, dt_raw, ab_dt_bias[i], ab_a_log[i], n_lat, n_inner)
            dskip = jnp.repeat(ab_d_skip[i, 0] + ab_d_skip[i, 1], SSM_HEAD_DIM)
            gn = _gated_norm(y, xbc, proj, dskip, ab_norm_w[i])
            wvec = jnp.concatenate([jnp.tile(ab_q_norm[i] * qk_scale, na_heads), jnp.tile(ab_k_norm[i], na_heads),
                                    jnp.ones((na_heads * HEAD_DIM,), F32)])
            qkvb = _prep(qkv, wvec, 2 * na_heads)
            bias = _na_bias_tiles(ab_rpb[i], n_lat // GRID_W)
            al = _neighbourhood_attention(qkvb, bias, n_heads=na_heads, n_lat=n_lat, n_ctx=n_ctx)
            mix_in = jnp.concatenate([gn, al], axis=1)
            w_out = ab_w_out[i].astype(BF16)
        else:
            p = _matmul(h, c_w_qkv[i].astype(BF16), name="c_in_qkv")
            wvec = jnp.concatenate([jnp.tile(c_q_norm[i] * qk_scale, attn_heads), jnp.tile(c_k_norm[i], kv_heads),
                                    jnp.ones((kv_heads * HEAD_DIM,), F32)])
            qkvb = _prep(p, wvec, attn_heads + kv_heads, attn_heads + kv_heads, cos, sin)
            mix_in = _attention(qkvb, n_heads=attn_heads, n_kv=kv_heads, n_lat=n_lat,
                                q_rows=n_lat if last else n_lat + n_ctx)
            w_out = c_w_out[i].astype(BF16)
        rows = n_lat if last else n_lat + n_ctx
        xr = _matmul(mix_in, w_out, rows=rows, epilogue="resid", res=xr, gate=mod[:, 2, :], n_lat=n_lat,
                     name="mixer_out")
        h2 = _normmod(xr, norm2_w[layer], mod[:, (4, 3), :], n_lat)
        a = _matmul(h2, w_mlp_in[layer].astype(BF16), out_dtype=BF16, epilogue="relu2", name="mlp_in")
        xr = _matmul(a, w_mlp_out[layer].astype(BF16), epilogue="resid", res=xr, gate=mod[:, 5, :], n_lat=n_lat,
                     name="mlp_out")
    return xr[None]
```

```python
import functools
import math

import numpy as np
import jax
import jax.numpy as jnp
from jax import lax
from jax.experimental import pallas as pl
from jax.experimental.pallas import tpu as pltpu

F32 = jnp.float32
BF16 = jnp.bfloat16

GRID_W = 64
HEAD_DIM = 128
ROPE_THETA = 10000.0
RMS_EPS = 1e-6
N_MOD = 6
SSM_HEAD_DIM = 64
SSM_GROUPS = 8
SSM_STATE = 128
SSM_NORM_GROUPS = 8
CONV_K = 5
NA_ROWS = 8
NA_COLS = 16

LANES = 128
SUBLANES = 8
VMEM_LIMIT_BYTES = 48 * 1024 * 1024

ROW_TILE = 256
SSD_CHUNK = 256
VT_ROWS = HEAD_DIM + 2 * SUBLANES
NA_QROWS = 4
NA_WROWS = 12


def _pick(n, candidates):
    for c in candidates:
        if n % c == 0:
            return c
    raise ValueError(f"no tile in {candidates} divides {n}")


def _cparams(sem):
    return pltpu.CompilerParams(dimension_semantics=sem, vmem_limit_bytes=VMEM_LIMIT_BYTES)


def _sigmoid(v):
    return 1.0 / (1.0 + jnp.exp(-v))


def _softplus(v):
    return jnp.maximum(v, 0.0) + jnp.log(1.0 + jnp.exp(-jnp.abs(v)))


def _dot(a, b):
    return jnp.dot(a, b, preferred_element_type=F32)


def _dot_nt(a, b):
    return lax.dot_general(a, b, (((1,), (1,)), ((), ())), preferred_element_type=F32)


def _split3(x):
    hi = x.astype(BF16)
    r1 = x - hi.astype(F32)
    mid = r1.astype(BF16)
    lo = (r1 - mid.astype(F32)).astype(BF16)
    return hi, mid, lo


def _mod_kernel(v_ref, w_ref, b_ref, o_ref):
    v = v_ref[...]
    s = (v * _sigmoid(v)).astype(BF16)
    o_ref[0] = _dot(s, w_ref[0].astype(BF16)) + b_ref[0]


def _modulation(vecs, w_mod, b_mod):
    depth, d, n = w_mod.shape
    tn = _pick(n, (1024, 512, 256, 128))
    return pl.pallas_call(
        _mod_kernel,
        grid=(depth, n // tn),
        in_specs=[pl.BlockSpec((SUBLANES, d), lambda l, j: (0, 0)),
                  pl.BlockSpec((1, d, tn), lambda l, j: (l, 0, j)),
                  pl.BlockSpec((1, 1, tn), lambda l, j: (l, 0, j))],
        out_specs=pl.BlockSpec((1, SUBLANES, tn), lambda l, j: (l, 0, j)),
        out_shape=jax.ShapeDtypeStruct((depth, SUBLANES, n), F32),
        compiler_params=_cparams(("arbitrary", "arbitrary")),
        name="modulation",
    )(vecs, w_mod, b_mod.reshape(depth, 1, n))


def _normmod_kernel(x_ref, w_ref, m_ref, o_ref):
    x = x_ref[...]
    y = x * lax.rsqrt(jnp.mean(x * x, axis=-1, keepdims=True) + RMS_EPS)
    m = m_ref[0]
    o_ref[...] = ((y * w_ref[...]) * (1.0 + m[0:1, :]) + m[1:2, :]).astype(o_ref.dtype)


def _normmod(x, w, scsh, n_lat):
    r, d = x.shape
    nlb = n_lat // ROW_TILE
    return pl.pallas_call(
        _normmod_kernel,
        grid=(r // ROW_TILE,),
        in_specs=[pl.BlockSpec((ROW_TILE, d), lambda i: (i, 0)),
                  pl.BlockSpec((1, d), lambda i: (0, 0)),
                  pl.BlockSpec((1, 2, d), lambda i: (i // nlb, 0, 0))],
        out_specs=pl.BlockSpec((ROW_TILE, d), lambda i: (i, 0)),
        out_shape=jax.ShapeDtypeStruct((r, d), BF16),
        compiler_params=_cparams(("arbitrary",)),
        name="normmod",
    )(x, w.reshape(1, d), scsh)


def _mm_kernel(*refs, nk, nk1, two_lhs, epilogue, n_lat, tm):
    refs = list(refs)
    x_ref = refs.pop(0)
    x2_ref = refs.pop(0) if two_lhs else None
    w_ref = refs.pop(0)
    if epilogue == "resid":
        res_ref = refs.pop(0)
        gate_ref = refs.pop(0)
    o_ref = refs.pop(0)

    def finish(acc):
        if epilogue == "relu2":
            a = jnp.maximum(acc, 0.0)
            o_ref[...] = (a * a).astype(o_ref.dtype)
        elif epilogue == "resid":
            row = pl.program_id(0) * tm + lax.broadcasted_iota(jnp.int32, (tm, 1), 0)
            g = jnp.where(row < n_lat, gate_ref[0:1, :], gate_ref[1:2, :])
            o_ref[...] = res_ref[...] + g * acc
        else:
            o_ref[...] = acc.astype(o_ref.dtype)

    if nk == 1:
        finish(_dot(x_ref[...], w_ref[0]))
        return
    acc_ref, = refs
    k = pl.program_id(2)

    @pl.when(k == 0)
    def _():
        acc_ref[...] = jnp.zeros_like(acc_ref)

    if two_lhs:
        @pl.when(k < nk1)
        def _():
            acc_ref[...] += _dot(x_ref[...], w_ref[0])

        @pl.when(k >= nk1)
        def _():
            acc_ref[...] += _dot(x2_ref[...], w_ref[0])
    else:
        acc_ref[...] += _dot(x_ref[...], w_ref[0])

    @pl.when(k == nk - 1)
    def _():
        finish(acc_ref[...])


def _matmul(x, w, layer, *, x2=None, rows=None, n_cols=None, out_dtype=F32, epilogue="none", res=None, gate=None,
            n_lat=0, name="matmul"):
    r = x.shape[0] if rows is None else rows
    _, kdim, n_w = w.shape
    n = n_w if n_cols is None else n_cols
    k1 = x.shape[1]
    tm = _pick(r, (1024, 768, 512, 256))
    tn = _pick(n, (1024, 512, 256, 128))
    if x2 is None:
        tk = kdim if kdim <= 2048 else _pick(kdim, (2048, 1024, 512))
    else:
        tk = math.gcd(k1, x2.shape[1])
    nk, nk1 = kdim // tk, k1 // tk
    in_specs = [pl.BlockSpec((tm, tk), lambda i, j, k: (i, jnp.minimum(k, nk1 - 1)))]
    args = [x]
    if x2 is not None:
        in_specs.append(pl.BlockSpec((tm, tk), lambda i, j, k: (i, jnp.maximum(k - nk1, 0))))
        args.append(x2)
    in_specs.append(pl.BlockSpec((1, tk, tn), lambda i, j, k: (layer, k, j)))
    args.append(w)
    if epilogue == "resid":
        in_specs += [pl.BlockSpec((tm, tn), lambda i, j, k: (i, j)),
                     pl.BlockSpec((2, tn), lambda i, j, k: (0, j))]
        args += [res, gate]
    kernel = functools.partial(_mm_kernel, nk=nk, nk1=nk1, two_lhs=x2 is not None, epilogue=epilogue, n_lat=n_lat,
                               tm=tm)
    return pl.pallas_call(
        kernel,
        grid=(r // tm, n // tn, nk),
        in_specs=in_specs,
        out_specs=pl.BlockSpec((tm, tn), lambda i, j, k: (i, j)),
        out_shape=jax.ShapeDtypeStruct((r, n), out_dtype),
        scratch_shapes=[] if nk == 1 else [pltpu.VMEM((tm, tn), F32)],
        compiler_params=_cparams(("arbitrary", "arbitrary", "arbitrary")),
        name=name,
    )(*args)


def _prep_kernel(*refs, n_norm, n_rope, n_blocks, n_vt):
    refs = list(refs)
    p_ref, w_ref = refs[:2]
    if n_rope:
        cos = refs[2][...]
        sin = refs[3][...]
        even = (lax.broadcasted_iota(jnp.int32, cos.shape, 1) % 2) == 0
    o_ref = refs[-2] if n_vt else refs[-1]
    for b in range(n_blocks):
        sl = slice(b * HEAD_DIM, (b + 1) * HEAD_DIM)
        u = p_ref[:, sl]
        if b < n_norm:
            u = u * lax.rsqrt(jnp.mean(u * u, axis=-1, keepdims=True) + RMS_EPS) * w_ref[:, sl]
        if b < n_rope:
            partner = jnp.where(even, pltpu.roll(u, HEAD_DIM - 1, 1), pltpu.roll(u, 1, 1))
            u = u * cos + partner * sin
        if b < n_blocks - n_vt:
            o_ref[:, sl] = u.astype(o_ref.dtype)
        else:
            vt_ref = refs[-1]
            hv = b - (n_blocks - n_vt)
            vt_ref[hv, 0, 0:HEAD_DIM, :] = u.T.astype(BF16)
            pad_row = lax.broadcasted_iota(jnp.int32, (VT_ROWS - HEAD_DIM, u.shape[0]), 0)
            vt_ref[hv, 0, HEAD_DIM:, :] = jnp.where(pad_row == 0, 1.0, 0.0).astype(BF16)


def _attn_key_tile(n_all):
    for tk in (768, ROW_TILE):
        if n_all % tk == 0 and (n_all // tk) % 2 == 1:
            return tk
    raise ValueError(f"no key tile for {n_all} rows")


def _prep(p, wvec, n_norm, n_rope=0, cos=None, sin=None, n_vt=0):
    r, n = p.shape
    n_blocks = n // HEAD_DIM
    n_out = (n_blocks - n_vt) * HEAD_DIM
    tk = _attn_key_tile(r) if n_vt else ROW_TILE
    per = tk // ROW_TILE
    in_specs = [pl.BlockSpec((ROW_TILE, n), lambda i: (i, 0)),
                pl.BlockSpec((1, n), lambda i: (0, 0))]
    args = [p, wvec.reshape(1, n)]
    if n_rope:
        in_specs += [pl.BlockSpec((ROW_TILE, HEAD_DIM), lambda i: (i, 0))] * 2
        args += [cos, sin]
    out_specs = [pl.BlockSpec((ROW_TILE, n_out), lambda i: (i, 0))]
    out_shape = [jax.ShapeDtypeStruct((r, n_out), BF16)]
    if n_vt:
        out_specs.append(pl.BlockSpec((n_vt, 1, VT_ROWS, ROW_TILE), lambda i: (0, i // per, 0, i % per)))
        out_shape.append(jax.ShapeDtypeStruct((n_vt, r // tk, VT_ROWS, tk), BF16))
    out = pl.pallas_call(
        functools.partial(_prep_kernel, n_norm=n_norm, n_rope=n_rope, n_blocks=n_blocks, n_vt=n_vt),
        grid=(r // ROW_TILE,),
        in_specs=in_specs,
        out_specs=out_specs,
        out_shape=out_shape,
        compiler_params=_cparams(("arbitrary",)),
        name="qkv_prep",
    )(*args)
    return out if n_vt else out[0]


def _attn_kernel(q_ref, k_ref, vt_ref, o_ref, s0_ref, s1_ref, m_ref, acc_ref, *, group, tq, tk, n_lat, n_all):
    n_tiles = n_all // tk
    n_ctx = n_all - n_lat
    m_ref[...] = jnp.full(m_ref.shape, -jnp.inf, F32)
    acc_ref[...] = jnp.zeros(acc_ref.shape, F32)

    def scores(kc, s_ref, g):
        s_ref[g, 0:kc.shape[0], :] = _dot_nt(kc, q_ref[:, g * HEAD_DIM:(g + 1) * HEAD_DIM])

    def consume(vtc, s_ref, g):
        st = s_ref[g, 0:vtc.shape[1], :]
        m_prev = m_ref[g]
        m_new = jnp.maximum(m_prev, jnp.max(st, axis=0, keepdims=True))
        p = jnp.exp2(st - m_new).astype(BF16)
        acc_ref[g] = jnp.exp2(m_prev - m_new) * acc_ref[g] + _dot(vtc, p)
        m_ref[g] = m_new

    def key_tile(c):
        return k_ref[pl.ds(pl.multiple_of(c * tk, tk), tk), :]

    def step(k_next, vt_cur, s_next, s_cur):
        for g in range(group):
            if k_next is not None:
                scores(k_next, s_next, g)
            if vt_cur is not None:
                consume(vt_cur, s_cur, g)

    def pair(i, carry):
        c = 2 * i
        step(key_tile(c + 1), vt_ref[0, c], s1_ref, s0_ref)
        step(key_tile(c + 2), vt_ref[0, c + 1], s0_ref, s1_ref)
        return carry

    @pl.when(pl.program_id(1) < n_lat // tq)
    def _():
        step(key_tile(0), None, s0_ref, None)
        lax.fori_loop(0, (n_tiles - 1) // 2, pair, 0)
        step(None, vt_ref[0, n_tiles - 1], None, s0_ref)

    @pl.when(pl.program_id(1) >= n_lat // tq)
    def _():
        step(k_ref[n_lat:n_all, :], None, s0_ref, None)
        step(None, vt_ref[0, n_tiles - 1, :, tk - n_ctx:tk], None, s0_ref)

    for g in range(group):
        out = acc_ref[g, 0:HEAD_DIM, :] / acc_ref[g, HEAD_DIM:HEAD_DIM + 1, :]
        o_ref[:, g * HEAD_DIM:(g + 1) * HEAD_DIM] = out.T.astype(o_ref.dtype)


def _attention(qk, vt, *, n_heads, n_kv, n_lat, q_rows):
    n_all = qk.shape[0]
    group = n_heads // n_kv
    tq = ROW_TILE
    tk = vt.shape[-1]
    gw = group * HEAD_DIM
    assert n_all - n_lat <= tk
    kernel = functools.partial(_attn_kernel, group=group, tq=tq, tk=tk, n_lat=n_lat, n_all=n_all)
    return pl.pallas_call(
        kernel,
        grid=(n_kv, q_rows // tq),
        in_specs=[pl.BlockSpec((tq, gw), lambda h, i: (i, h)),
                  pl.BlockSpec((n_all, HEAD_DIM), lambda h, i: (0, n_heads + h)),
                  pl.BlockSpec((1, n_all // tk, VT_ROWS, tk), lambda h, i: (h, 0, 0, 0))],
        out_specs=pl.BlockSpec((tq, gw), lambda h, i: (i, h)),
        out_shape=jax.ShapeDtypeStruct((q_rows, n_heads * HEAD_DIM), BF16),
        scratch_shapes=[pltpu.VMEM((group, tk, tq), F32),
                        pltpu.VMEM((group, tk, tq), F32),
                        pltpu.VMEM((group, 1, tq), F32),
                        pltpu.VMEM((group, VT_ROWS, tq), F32)],
        compiler_params=_cparams(("arbitrary", "arbitrary")),
        name="gqa_attention",
    )(qk, qk, vt)


def _na_kernel(q_ref, k_ref, v_ref, bias_ref, o_ref, *, n_lat, rows, n_ctx):
    nb = rows // NA_QROWS
    b = pl.program_id(1)
    q = q_ref[...]
    kctx = k_ref[n_lat:n_lat + n_ctx, :]
    vctx = v_ref[n_lat:n_lat + n_ctx, :]
    s_c = _dot_nt(q, kctx)

    @pl.when(b < nb)
    def _():
        w0 = jnp.clip(NA_QROWS * b - NA_ROWS // 2, 0, rows - NA_WROWS)
        start = pl.multiple_of(w0 * GRID_W, GRID_W)
        kw = k_ref[pl.ds(start, NA_WROWS * GRID_W), :]
        vw = v_ref[pl.ds(start, NA_WROWS * GRID_W), :]
        s_w = _dot_nt(q, kw) + bias_ref[0, 0]
        m = jnp.maximum(jnp.max(s_w, axis=1, keepdims=True), jnp.max(s_c, axis=1, keepdims=True))
        p_w = jnp.exp2(s_w - m)
        p_c = jnp.exp2(s_c - m)
        l = jnp.sum(p_w, axis=1, keepdims=True) + jnp.sum(p_c, axis=1, keepdims=True)
        o = _dot(p_w.astype(BF16), vw) + _dot(p_c.astype(BF16), vctx)
        o_ref[...] = (o / l).astype(o_ref.dtype)

    @pl.when(b >= nb)
    def _():
        m = jnp.max(s_c, axis=1, keepdims=True)
        p_c = jnp.exp2(s_c - m)
        l = jnp.sum(p_c, axis=1, keepdims=True)
        o_ref[...] = (_dot(p_c.astype(BF16), vctx) / l).astype(o_ref.dtype)


def _na_bias_tiles(rpb, rows):
    h = rpb.shape[0]
    col = np.arange(GRID_W)
    c0 = np.clip(col - NA_COLS // 2, 0, GRID_W - NA_COLS)
    col_ok = (col[None, :] >= c0[:, None]) & (col[None, :] < c0[:, None] + NA_COLS)
    pad = GRID_W - NA_COLS
    rp = jnp.pad(rpb * math.log2(math.e), ((0, 0), (0, 0), (pad, pad)))
    toep = jnp.stack([rp[:, :, NA_COLS - 1 - q + pad:NA_COLS - 1 - q + pad + GRID_W] for q in range(GRID_W)], axis=2)
    toep = jnp.where(col_ok[None, None], toep, -jnp.inf)
    neg = jnp.full((h, GRID_W, GRID_W), -jnp.inf, F32)
    half = NA_ROWS // 2
    kinds = (lambda j: (j, 0), lambda j: (half + j, j), lambda j: (NA_ROWS + j, half))
    tiles = []
    for kind in kinds:
        qrows = []
        for j in range(NA_QROWS):
            q_row, win = kind(j)
            blocks = []
            for ik in range(NA_WROWS):
                dr = ik - q_row + (NA_ROWS - 1)
                ok = win <= ik < win + NA_ROWS
                blocks.append(toep[:, dr] if ok else neg)
            qrows.append(jnp.concatenate(blocks, axis=-1))
        tiles.append(jnp.concatenate(qrows, axis=-2))
    return jnp.stack(tiles, axis=0)


def _neighbourhood_attention(qkv, bias, *, n_heads, n_lat, n_ctx):
    r = qkv.shape[0]
    rows = n_lat // GRID_W
    nb = rows // NA_QROWS
    tq = NA_QROWS * GRID_W
    assert tq == n_ctx and rows >= NA_WROWS + NA_QROWS

    def bias_map(h, b):
        return (jnp.where(b == 0, 0, jnp.where(b >= nb - 1, 2, 1)), h, 0, 0)

    kernel = functools.partial(_na_kernel, n_lat=n_lat, rows=rows, n_ctx=n_ctx)
    return pl.pallas_call(
        kernel,
        grid=(n_heads, r // tq),
        in_specs=[pl.BlockSpec((tq, HEAD_DIM), lambda h, b: (b, h)),
                  pl.BlockSpec((r, HEAD_DIM), lambda h, b: (0, n_heads + h)),
                  pl.BlockSpec((r, HEAD_DIM), lambda h, b: (0, 2 * n_heads + h)),
                  pl.BlockSpec((1, 1, tq, NA_WROWS * GRID_W), bias_map)],
        out_specs=pl.BlockSpec((tq, HEAD_DIM), lambda h, b: (b, h)),
        out_shape=jax.ShapeDtypeStruct((r, n_heads * HEAD_DIM), BF16),
        compiler_params=_cparams(("arbitrary", "arbitrary")),
        name="neighbourhood_attention",
    )(qkv, qkv, qkv, bias)


def _conv_kernel(u_ref, prev_ref, next_ref, w_ref, b_ref, o_ref, ext_ref, *, n_lat):
    tr = u_ref.shape[0]
    i = pl.program_id(0)
    nlb = n_lat // tr
    has_prev = jnp.logical_and(i != 0, i != nlb)
    has_next = i < nlb - 1
    ext_ref[0:SUBLANES, :] = jnp.where(has_prev, prev_ref[...], 0.0)
    ext_ref[SUBLANES:SUBLANES + tr, :] = u_ref[...]
    ext_ref[SUBLANES + tr:, :] = jnp.where(has_next, next_ref[...], 0.0)
    acc = jnp.broadcast_to(b_ref[...], o_ref.shape)
    for j in range(CONV_K):
        off = SUBLANES - CONV_K // 2 + j
        acc = acc + w_ref[j:j + 1, :] * ext_ref[off:off + tr, :]
    o_ref[...] = acc * _sigmoid(acc)


def _conv_silu(proj, col0, conv_w, conv_b, n_lat):
    r = proj.shape[0]
    ch = conv_w.shape[1]
    tr = ROW_TILE
    tc = _pick(ch, (1024, 512, 256, 128))
    cb0 = col0 // tc
    assert col0 % tc == 0
    rb = tr // SUBLANES
    last = r // SUBLANES - 1
    return pl.pallas_call(
        functools.partial(_conv_kernel, n_lat=n_lat),
        grid=(r // tr, ch // tc),
        in_specs=[pl.BlockSpec((tr, tc), lambda i, j: (i, cb0 + j)),
                  pl.BlockSpec((SUBLANES, tc), lambda i, j: (jnp.maximum(i * rb - 1, 0), cb0 + j)),
                  pl.BlockSpec((SUBLANES, tc), lambda i, j: (jnp.minimum((i + 1) * rb, last), cb0 + j)),
                  pl.BlockSpec((CONV_K, tc), lambda i, j: (0, j)),
                  pl.BlockSpec((1, tc), lambda i, j: (0, j))],
        out_specs=pl.BlockSpec((tr, tc), lambda i, j: (i, j)),
        out_shape=jax.ShapeDtypeStruct((r, ch), F32),
        scratch_shapes=[pltpu.VMEM((tr + 2 * SUBLANES, tc), F32)],
        compiler_params=_cparams(("arbitrary", "arbitrary")),
        name="conv_silu",
    )(proj, proj, proj, conv_w, conv_b.reshape(1, ch))


def _ssd_kernel(xs_ref, b_ref, c_ref, dt_ref, dtt_ref, bias_ref, biast_ref, alog_ref, alogt_ref, e_ref,
                y_ref, st_ref, *, n_groups, heads_per_group):
    t = xs_ref.shape[0]
    d = pl.program_id(0)
    c = pl.program_id(1)
    gw = heads_per_group * SSM_HEAD_DIM

    @pl.when(c == 0)
    def _():
        st_ref[...] = jnp.zeros(st_ref.shape, F32)

    dt = _softplus(dt_ref[0] + bias_ref[0])
    dtt = _softplus(dtt_ref[0] + biast_ref[0])
    da = dt * (-jnp.exp(alog_ref[0]))
    dat = dtt * (-jnp.exp(alogt_ref[0]))
    row = lax.broadcasted_iota(jnp.int32, (t, t), 0)
    col = lax.broadcasted_iota(jnp.int32, (t, t), 1)
    lag = jnp.where(d == 0, row - col, col - row)
    valid = lag >= 0
    ones_v = jnp.where(valid, 1.0, 0.0).astype(BF16)
    ones_vt = jnp.where(lag <= 0, 1.0, 0.0).astype(BF16)
    cum = sum(_dot(ones_v, piece) for piece in _split3(da))
    cumt = sum(_dot(piece, ones_vt) for piece in _split3(dat))
    tot = jnp.sum(da, axis=0, keepdims=True)
    cumt_in = cumt - jnp.log(dtt)
    stack = jnp.concatenate([dt * jnp.exp(tot - cum), jnp.exp(cum),
                             jnp.broadcast_to(jnp.exp(tot), (SUBLANES, tot.shape[1]))], axis=0)
    wide = _dot(jnp.concatenate(_split3(stack), axis=1), e_ref[...])
    win_w = wide[0:t]
    ecum_w = wide[t:2 * t]
    cdec_w = wide[2 * t:2 * t + 1]
    xs = xs_ref[...]
    xin = (xs * win_w).astype(BF16)
    lane_head = lax.broadcasted_iota(jnp.int32, (t, gw), 1) // SSM_HEAD_DIM

    for g in range(n_groups):
        ns = slice(g * SSM_STATE, (g + 1) * SSM_STATE)
        hs = slice(g * gw, (g + 1) * gw)
        bg = b_ref[:, ns]
        cg = c_ref[:, ns].astype(BF16)
        cb = _dot_nt(cg, bg.astype(BF16))
        acc = _dot(cg, st_ref[g].astype(BF16)) * ecum_w[:, hs]
        xs_g = xs[:, hs]
        for k in range(heads_per_group):
            h = g * heads_per_group + k
            seg = cum[:, h:h + 1] - cumt_in[h:h + 1, :]
            decay = jnp.exp(jnp.where(valid, seg, -jnp.inf))
            mh = (cb * decay).astype(BF16)
            xk = jnp.where(lane_head == k, xs_g, 0.0).astype(BF16)
            acc = acc + _dot(mh, xk)
        y_ref[0, :, hs] = acc
        st_ref[g] = st_ref[g] * cdec_w[:, hs] + _dot(bg.T.astype(BF16), xin[:, hs])


def _ssd(xbc, dt_raw, dt_bias, a_log, n_lat, n_inner):
    r = xbc.shape[0]
    t = SSD_CHUNK
    heads = n_inner // SSM_HEAD_DIM
    gn = SSM_GROUPS * SSM_STATE
    hpg = heads // SSM_GROUPS
    n_lc = n_lat // t
    assert r - n_lat == t and n_inner % gn == 0
    dt3 = dt_raw[:, :2 * heads].reshape(r, 2, heads).transpose(1, 0, 2)
    dt3t = dt3.transpose(0, 2, 1)
    expand = jnp.asarray(np.tile(np.repeat(np.eye(heads, dtype=np.float32), SSM_HEAD_DIM, axis=1), (3, 1)), BF16)

    def chunk(d, c):
        return jnp.where(c == 0, n_lc, jnp.where(d == 0, c - 1, n_lc - c))

    xb = n_inner // gn
    kernel = functools.partial(_ssd_kernel, n_groups=SSM_GROUPS, heads_per_group=hpg)
    return pl.pallas_call(
        kernel,
        grid=(2, n_lc + 1),
        in_specs=[pl.BlockSpec((t, n_inner), lambda d, c: (chunk(d, c), 0)),
                  pl.BlockSpec((t, gn), lambda d, c: (chunk(d, c), xb)),
                  pl.BlockSpec((t, gn), lambda d, c: (chunk(d, c), xb + 1)),
                  pl.BlockSpec((1, t, heads), lambda d, c: (d, chunk(d, c), 0)),
                  pl.BlockSpec((1, heads, t), lambda d, c: (d, 0, chunk(d, c))),
                  pl.BlockSpec((1, 1, heads), lambda d, c: (d, 0, 0)),
                  pl.BlockSpec((1, heads, 1), lambda d, c: (d, 0, 0)),
                  pl.BlockSpec((1, 1, heads), lambda d, c: (d, 0, 0)),
                  pl.BlockSpec((1, heads, 1), lambda d, c: (d, 0, 0)),
                  pl.BlockSpec((3 * heads, n_inner), lambda d, c: (0, 0))],
        out_specs=pl.BlockSpec((1, t, n_inner), lambda d, c: (d, chunk(d, c), 0)),
        out_shape=jax.ShapeDtypeStruct((2, r, n_inner), F32),
        scratch_shapes=[pltpu.VMEM((SSM_GROUPS, SSM_STATE, hpg * SSM_HEAD_DIM), F32)],
        compiler_params=_cparams(("arbitrary", "arbitrary")),
        name="ssd_scan",
    )(xbc, xbc, xbc, dt3, dt3t, dt_bias.reshape(2, 1, heads), dt_bias.reshape(2, heads, 1),
      a_log.reshape(2, 1, heads), a_log.reshape(2, heads, 1), expand)


def _gated_norm_kernel(y_ref, xs_ref, z_ref, dskip_ref, w_ref, o_ref, *, n_groups):
    z = z_ref[...]
    u = (y_ref[0] + y_ref[1] + dskip_ref[...] * xs_ref[...]) * (z * _sigmoid(z))
    gw = u.shape[1] // n_groups
    for g in range(n_groups):
        sl = slice(g * gw, (g + 1) * gw)
        ug = u[:, sl]
        ug = ug * lax.rsqrt(jnp.mean(ug * ug, axis=-1, keepdims=True) + RMS_EPS)
        o_ref[:, sl] = (ug * w_ref[:, sl]).astype(o_ref.dtype)


def _gated_norm(y, xbc, proj, dskip_w, norm_w):
    _, r, n = y.shape
    return pl.pallas_call(
        functools.partial(_gated_norm_kernel, n_groups=SSM_NORM_GROUPS),
        grid=(r // ROW_TILE,),
        in_specs=[pl.BlockSpec((2, ROW_TILE, n), lambda i: (0, i, 0)),
                  pl.BlockSpec((ROW_TILE, n), lambda i: (i, 0)),
                  pl.BlockSpec((ROW_TILE, n), lambda i: (i, 0)),
                  pl.BlockSpec((1, n), lambda i: (0, 0)),
                  pl.BlockSpec((1, n), lambda i: (0, 0))],
        out_specs=pl.BlockSpec((ROW_TILE, n), lambda i: (i, 0)),
        out_shape=jax.ShapeDtypeStruct((r, n), BF16),
        compiler_params=_cparams(("arbitrary",)),
        name="gated_norm",
    )(y, xbc, proj, dskip_w.reshape(1, n), norm_w.reshape(1, n))


def _rope_tables(n_lat, n_ctx):
    t = jnp.arange(n_lat)
    half = HEAD_DIM // 2
    inv = 1.0 / (ROPE_THETA ** (jnp.arange(0, half, 2, dtype=F32) / half))
    ang = jnp.concatenate([(t // GRID_W).astype(F32)[:, None] * inv,
                           (t % GRID_W).astype(F32)[:, None] * inv], axis=-1)
    cos = jnp.repeat(jnp.cos(ang), 2, axis=-1)
    sin = jnp.repeat(jnp.sin(ang), 2, axis=-1) * jnp.asarray(np.tile(np.array([-1.0, 1.0], np.float32), half))
    cos = jnp.concatenate([cos, jnp.ones((n_ctx, HEAD_DIM), F32)], axis=0)
    sin = jnp.concatenate([sin, jnp.zeros((n_ctx, HEAD_DIM), F32)], axis=0)
    return cos, sin


def kernel(x, c, ctx, c_ctx, w_mod, b_mod, norm1_w, norm2_w, w_mlp_in, w_mlp_out, ab_w_in, ab_conv_w, ab_conv_b,
           ab_dt_bias, ab_a_log, ab_d_skip, ab_norm_w, ab_q_norm, ab_k_norm, ab_rpb, ab_w_out, c_w_qkv, c_q_norm,
           c_k_norm, c_w_out):
    _, n_lat, d = x.shape
    n_ctx = ctx.shape[1]
    depth = w_mod.shape[0]
    assert x.shape[0] == 1 and n_ctx == ROW_TILE and n_lat % 1024 == 0
    n_inner = d
    ssm_heads = n_inner // SSM_HEAD_DIM
    conv_dim = ab_conv_w.shape[-1]
    na_heads = ab_rpb.shape[1]
    attn_heads = c_w_out.shape[1] // HEAD_DIM
    kv_heads = (c_w_qkv.shape[-1] // HEAD_DIM - attn_heads) // 2
    qk_scale = HEAD_DIM ** -0.5 * math.log2(math.e)
    i0 = n_inner
    i1 = i0 + conv_dim
    i2 = i1 + 2 * ssm_heads

    w_mlp_in_b = w_mlp_in.astype(BF16)
    w_mlp_out_b = w_mlp_out.astype(BF16)
    ab_w_in_b = ab_w_in.astype(BF16)
    ab_w_dt_b = jnp.pad(ab_w_in[:, :, i1:i2], ((0, 0), (0, 0), (0, LANES - (i2 - i1)))).astype(BF16)
    ab_w_qkv_b = ab_w_in[:, :, i2:].astype(BF16)
    ab_w_out_b = ab_w_out.astype(BF16)
    c_w_qkv_b = c_w_qkv.astype(BF16)
    c_w_out_b = c_w_out.astype(BF16)

    xr = jnp.concatenate([x[0], ctx[0]], axis=0)
    vecs = jnp.zeros((SUBLANES, d), F32).at[0].set(c[0]).at[1].set(c_ctx)
    mods = _modulation(vecs, w_mod, b_mod)[:, :2].reshape(depth, 2, N_MOD, d)
    cos, sin = _rope_tables(n_lat, n_ctx)

    for layer in range(depth):
        last = layer == depth - 1
        i = layer // 2
        mod = mods[layer]
        scsh1 = jnp.stack([mod[:, 1], mod[:, 0]], axis=1)
        scsh2 = jnp.stack([mod[:, 4], mod[:, 3]], axis=1)
        rows = n_lat if last else n_lat + n_ctx
        h = _normmod(xr, norm1_w[layer], scsh1, n_lat)
        if layer % 2 == 0:
            proj = _matmul(h, ab_w_in_b, i, n_cols=i1, name="ab_in_zxbc")
            dt_raw = _matmul(h, ab_w_dt_b, i, name="ab_in_dt")
            qkv = _matmul(h, ab_w_qkv_b, i, name="ab_in_qkv")
            xbc = _conv_silu(proj, i0, ab_conv_w[i], ab_conv_b[i], n_lat)
            y = _ssd(xbc, dt_raw, ab_dt_bias[i], ab_a_log[i], n_lat, n_inner)
            dskip = jnp.repeat(ab_d_skip[i, 0] + ab_d_skip[i, 1], SSM_HEAD_DIM)
            gn = _gated_norm(y, xbc, proj, dskip, ab_norm_w[i])
            wvec = jnp.concatenate([jnp.tile(ab_q_norm[i] * qk_scale, na_heads), jnp.tile(ab_k_norm[i], na_heads),
                                    jnp.ones((na_heads * HEAD_DIM,), F32)])
            qkvb = _prep(qkv, wvec, 2 * na_heads)
            bias = _na_bias_tiles(ab_rpb[i], n_lat // GRID_W)
            al = _neighbourhood_attention(qkvb, bias, n_heads=na_heads, n_lat=n_lat, n_ctx=n_ctx)
            xr = _matmul(gn, ab_w_out_b, i, x2=al, rows=rows, epilogue="resid", res=xr, gate=mod[:, 2, :],
                         n_lat=n_lat, name="mixer_out")
        else:
            p = _matmul(h, c_w_qkv_b, i, name="c_in_qkv")
            wvec = jnp.concatenate([jnp.tile(c_q_norm[i] * qk_scale, attn_heads), jnp.tile(c_k_norm[i], kv_heads),
                                    jnp.ones((kv_heads * HEAD_DIM,), F32)])
            qk, vt = _prep(p, wvec, attn_heads + kv_heads, attn_heads + kv_heads, cos, sin, n_vt=kv_heads)
            att = _attention(qk, vt, n_heads=attn_heads, n_kv=kv_heads, n_lat=n_lat, q_rows=rows)
            xr = _matmul(att, c_w_out_b, i, rows=rows, epilogue="resid", res=xr, gate=mod[:, 2, :], n_lat=n_lat,
                         name="mixer_out")
        h2 = _normmod(xr, norm2_w[layer], scsh2, n_lat)
        a = _matmul(h2, w_mlp_in_b, layer, out_dtype=BF16, epilogue="relu2", name="mlp_in")
        xr = _matmul(a, w_mlp_out_b, layer, epilogue="resid", res=xr, gate=mod[:, 5, :], n_lat=n_lat,
                     name="mlp_out")
    return xr[None]
```

```python
import functools
import math

import numpy as np
import jax
import jax.numpy as jnp
from jax import lax
from jax.experimental import pallas as pl
from jax.experimental.pallas import tpu as pltpu

F32 = jnp.float32
BF16 = jnp.bfloat16

GRID_W = 64
HEAD_DIM = 128
ROPE_THETA = 10000.0
RMS_EPS = 1e-6
N_MOD = 6
SSM_HEAD_DIM = 64
SSM_GROUPS = 8
SSM_STATE = 128
SSM_NORM_GROUPS = 8
CONV_K = 5
NA_ROWS = 8
NA_COLS = 16

LANES = 128
SUBLANES = 8
VMEM_LIMIT_BYTES = 48 * 1024 * 1024
VMEM_LIMIT_WEIGHT_RESIDENT = 56 * 1024 * 1024

ROW_TILE = 256
SSD_CHUNK = 256
VT_ROWS = HEAD_DIM + 2 * SUBLANES
NA_QROWS = 4
NA_WROWS = 12
NA_HEADS_PER_STEP = 2


def _pick(n, candidates):
    for c in candidates:
        if n % c == 0:
            return c
    raise ValueError(f"no tile in {candidates} divides {n}")


def _cparams(sem):
    return pltpu.CompilerParams(dimension_semantics=sem, vmem_limit_bytes=VMEM_LIMIT_BYTES)


def _sigmoid(v):
    return 1.0 / (1.0 + jnp.exp(-v))


def _softplus(v):
    return jnp.maximum(v, 0.0) + jnp.log(1.0 + jnp.exp(-jnp.abs(v)))


def _dot(a, b):
    return jnp.dot(a, b, preferred_element_type=F32)


def _dot_nt(a, b):
    return lax.dot_general(a, b, (((1,), (1,)), ((), ())), preferred_element_type=F32)


def _split3(x):
    hi = x.astype(BF16)
    r1 = x - hi.astype(F32)
    mid = r1.astype(BF16)
    lo = (r1 - mid.astype(F32)).astype(BF16)
    return hi, mid, lo


def _mod_kernel(v_ref, w_ref, b_ref, o_ref):
    v = v_ref[...]
    s = (v * _sigmoid(v)).astype(BF16)
    o_ref[0] = _dot(s, w_ref[0].astype(BF16)) + b_ref[0]


def _modulation(vecs, w_mod, b_mod):
    depth, d, n = w_mod.shape
    tn = _pick(n, (1024, 512, 256, 128))
    return pl.pallas_call(
        _mod_kernel,
        grid=(depth, n // tn),
        in_specs=[pl.BlockSpec((SUBLANES, d), lambda l, j: (0, 0)),
                  pl.BlockSpec((1, d, tn), lambda l, j: (l, 0, j)),
                  pl.BlockSpec((1, 1, tn), lambda l, j: (l, 0, j))],
        out_specs=pl.BlockSpec((1, SUBLANES, tn), lambda l, j: (l, 0, j)),
        out_shape=jax.ShapeDtypeStruct((depth, SUBLANES, n), F32),
        compiler_params=_cparams(("arbitrary", "arbitrary")),
        name="modulation",
    )(vecs, w_mod, b_mod.reshape(depth, 1, n))


def _normmod_kernel(x_ref, w_ref, m_ref, o_ref):
    x = x_ref[...]
    y = x * lax.rsqrt(jnp.mean(x * x, axis=-1, keepdims=True) + RMS_EPS)
    m = m_ref[0]
    o_ref[...] = ((y * w_ref[...]) * (1.0 + m[0:1, :]) + m[1:2, :]).astype(o_ref.dtype)


def _normmod(x, w, scsh, n_lat):
    r, d = x.shape
    nlb = n_lat // ROW_TILE
    return pl.pallas_call(
        _normmod_kernel,
        grid=(r // ROW_TILE,),
        in_specs=[pl.BlockSpec((ROW_TILE, d), lambda i: (i, 0)),
                  pl.BlockSpec((1, d), lambda i: (0, 0)),
                  pl.BlockSpec((1, 2, d), lambda i: (i // nlb, 0, 0))],
        out_specs=pl.BlockSpec((ROW_TILE, d), lambda i: (i, 0)),
        out_shape=jax.ShapeDtypeStruct((r, d), BF16),
        compiler_params=_cparams(("arbitrary",)),
        name="normmod",
    )(x, w.reshape(1, d), scsh)


def _mm_kernel(*refs, nk, nk1, two_lhs, epilogue, n_lat, tm):
    refs = list(refs)
    x_ref = refs.pop(0)
    x2_ref = refs.pop(0) if two_lhs else None
    w_ref = refs.pop(0)
    if epilogue == "resid":
        res_ref = refs.pop(0)
        gate_ref = refs.pop(0)
    o_ref = refs.pop(0)

    def finish(acc):
        if epilogue == "relu2":
            a = jnp.maximum(acc, 0.0)
            o_ref[...] = (a * a).astype(o_ref.dtype)
        elif epilogue == "resid":
            row = pl.program_id(0) * tm + lax.broadcasted_iota(jnp.int32, (tm, 1), 0)
            g = jnp.where(row < n_lat, gate_ref[0:1, :], gate_ref[1:2, :])
            o_ref[...] = res_ref[...] + g * acc
        else:
            o_ref[...] = acc.astype(o_ref.dtype)

    if nk == 1:
        finish(_dot(x_ref[...], w_ref[0]))
        return
    acc_ref, = refs
    k = pl.program_id(2)

    @pl.when(k == 0)
    def _():
        acc_ref[...] = jnp.zeros_like(acc_ref)

    if two_lhs:
        @pl.when(k < nk1)
        def _():
            acc_ref[...] += _dot(x_ref[...], w_ref[0])

        @pl.when(k >= nk1)
        def _():
            acc_ref[...] += _dot(x2_ref[...], w_ref[0])
    else:
        acc_ref[...] += _dot(x_ref[...], w_ref[0])

    @pl.when(k == nk - 1)
    def _():
        finish(acc_ref[...])


def _matmul(x, w, layer, *, x2=None, rows=None, n_cols=None, out_dtype=F32, epilogue="none", res=None, gate=None,
            n_lat=0, name="matmul"):
    r = x.shape[0] if rows is None else rows
    _, kdim, n_w = w.shape
    n = n_w if n_cols is None else n_cols
    k1 = x.shape[1]
    tm = _pick(r, (1024, 768, 512, 256))
    tn = _pick(n, (1024, 512, 256, 128))
    if x2 is None:
        tk = kdim if kdim <= 2048 else _pick(kdim, (2048, 1024, 512))
    else:
        tk = math.gcd(k1, x2.shape[1])
    nk, nk1 = kdim // tk, k1 // tk
    in_specs = [pl.BlockSpec((tm, tk), lambda i, j, k: (i, jnp.minimum(k, nk1 - 1)))]
    args = [x]
    if x2 is not None:
        in_specs.append(pl.BlockSpec((tm, tk), lambda i, j, k: (i, jnp.maximum(k - nk1, 0))))
        args.append(x2)
    in_specs.append(pl.BlockSpec((1, tk, tn), lambda i, j, k: (layer, k, j)))
    args.append(w)
    if epilogue == "resid":
        in_specs += [pl.BlockSpec((tm, tn), lambda i, j, k: (i, j)),
                     pl.BlockSpec((2, tn), lambda i, j, k: (0, j))]
        args += [res, gate]
    kernel = functools.partial(_mm_kernel, nk=nk, nk1=nk1, two_lhs=x2 is not None, epilogue=epilogue, n_lat=n_lat,
                               tm=tm)
    return pl.pallas_call(
        kernel,
        grid=(r // tm, n // tn, nk),
        in_specs=in_specs,
        out_specs=pl.BlockSpec((tm, tn), lambda i, j, k: (i, j)),
        out_shape=jax.ShapeDtypeStruct((r, n), out_dtype),
        scratch_shapes=[] if nk == 1 else [pltpu.VMEM((tm, tn), F32)],
        compiler_params=_cparams(("arbitrary", "arbitrary", "arbitrary")),
        name=name,
    )(*args)


def _mmw_kernel(*refs, prologue, two_lhs, epilogue, n_lat, tm, k1):
    refs = list(refs)
    x_ref = refs.pop(0)
    if prologue:
        nw_ref = refs.pop(0)
        mod_ref = refs.pop(0)
    x2_ref = refs.pop(0) if two_lhs else None
    w_ref = refs.pop(0)
    if epilogue == "resid":
        res_ref = refs.pop(0)
        gate_ref = refs.pop(0)
    o_ref, wb_ref = refs
    i = pl.program_id(1)

    @pl.when(i == 0)
    def _():
        wb_ref[...] = w_ref[0].astype(BF16)

    is_lat = (i * tm + lax.broadcasted_iota(jnp.int32, (tm, 1), 0)) < n_lat
    if prologue:
        x = x_ref[...]
        y = x * lax.rsqrt(jnp.mean(x * x, axis=-1, keepdims=True) + RMS_EPS)
        scale = jnp.where(is_lat, mod_ref[0, 0:1, :], mod_ref[1, 0:1, :])
        shift = jnp.where(is_lat, mod_ref[0, 1:2, :], mod_ref[1, 1:2, :])
        h = ((y * nw_ref[...]) * (1.0 + scale) + shift).astype(BF16)
    else:
        h = x_ref[...]
    if two_lhs:
        acc = _dot(h, wb_ref[0:k1, :]) + _dot(x2_ref[...], wb_ref[k1:, :])
    else:
        acc = _dot(h, wb_ref[...])
    if epilogue == "relu2":
        a = jnp.maximum(acc, 0.0)
        o_ref[...] = (a * a).astype(o_ref.dtype)
    elif epilogue == "resid":
        o_ref[...] = res_ref[...] + jnp.where(is_lat, gate_ref[0:1, :], gate_ref[1:2, :]) * acc
    else:
        o_ref[...] = acc.astype(o_ref.dtype)


def _matmul_w(x, w, layer, *, norm=None, x2=None, rows=None, n_cols=None, tn=1024, out_dtype=F32, epilogue="none",
              res=None, gate=None, n_lat=0, name="matmul_w"):
    r = x.shape[0] if rows is None else rows
    _, kdim, n_w = w.shape
    n = n_w if n_cols is None else n_cols
    k1 = x.shape[1]
    tm = _pick(r, (1024, 768, 512, 256))
    in_specs = [pl.BlockSpec((tm, k1), lambda j, i: (i, 0))]
    args = [x]
    if norm is not None:
        in_specs += [pl.BlockSpec((1, k1), lambda j, i: (0, 0)),
                     pl.BlockSpec((2, 2, k1), lambda j, i: (0, 0, 0))]
        args += [norm[0].reshape(1, k1), norm[1]]
    if x2 is not None:
        in_specs.append(pl.BlockSpec((tm, x2.shape[1]), lambda j, i: (i, 0)))
        args.append(x2)
    in_specs.append(pl.BlockSpec((1, kdim, tn), lambda j, i: (layer, 0, j)))
    args.append(w)
    if epilogue == "resid":
        in_specs += [pl.BlockSpec((tm, tn), lambda j, i: (i, j)),
                     pl.BlockSpec((2, tn), lambda j, i: (0, j))]
        args += [res, gate]
    kernel = functools.partial(_mmw_kernel, prologue=norm is not None, two_lhs=x2 is not None, epilogue=epilogue,
                               n_lat=n_lat, tm=tm, k1=k1)
    return pl.pallas_call(
        kernel,
        grid=(n // tn, r // tm),
        in_specs=in_specs,
        out_specs=pl.BlockSpec((tm, tn), lambda j, i: (i, j)),
        out_shape=jax.ShapeDtypeStruct((r, n), out_dtype),
        scratch_shapes=[pltpu.VMEM((kdim, tn), BF16)],
        compiler_params=pltpu.CompilerParams(dimension_semantics=("arbitrary", "arbitrary"),
                                             vmem_limit_bytes=VMEM_LIMIT_WEIGHT_RESIDENT),
        name=name,
    )(*args)


def _prep_kernel(*refs, n_norm, n_rope, n_blocks, n_vt):
    refs = list(refs)
    p_ref, w_ref = refs[:2]
    if n_rope:
        cos = refs[2][...]
        sin = refs[3][...]
        even = (lax.broadcasted_iota(jnp.int32, cos.shape, 1) % 2) == 0
    o_ref = refs[-2] if n_vt else refs[-1]
    for b in range(n_blocks):
        sl = slice(b * HEAD_DIM, (b + 1) * HEAD_DIM)
        u = p_ref[:, sl]
        if b < n_norm:
            u = u * lax.rsqrt(jnp.mean(u * u, axis=-1, keepdims=True) + RMS_EPS) * w_ref[:, sl]
        if b < n_rope:
            partner = jnp.where(even, pltpu.roll(u, HEAD_DIM - 1, 1), pltpu.roll(u, 1, 1))
            u = u * cos + partner * sin
        if b < n_blocks - n_vt:
            o_ref[:, sl] = u.astype(o_ref.dtype)
        else:
            vt_ref = refs[-1]
            hv = b - (n_blocks - n_vt)
            vt_ref[hv, 0, 0:HEAD_DIM, :] = u.T.astype(BF16)
            pad_row = lax.broadcasted_iota(jnp.int32, (VT_ROWS - HEAD_DIM, u.shape[0]), 0)
            vt_ref[hv, 0, HEAD_DIM:, :] = jnp.where(pad_row == 0, 1.0, 0.0).astype(BF16)


def _attn_key_tile(n_all):
    for tk in (768, ROW_TILE):
        if n_all % tk == 0 and (n_all // tk) % 2 == 1:
            return tk
    raise ValueError(f"no key tile for {n_all} rows")


def _prep(p, wvec, n_norm, n_rope=0, cos=None, sin=None, n_vt=0):
    r, n = p.shape
    n_blocks = n // HEAD_DIM
    n_out = (n_blocks - n_vt) * HEAD_DIM
    tk = _attn_key_tile(r) if n_vt else ROW_TILE
    per = tk // ROW_TILE
    in_specs = [pl.BlockSpec((ROW_TILE, n), lambda i: (i, 0)),
                pl.BlockSpec((1, n), lambda i: (0, 0))]
    args = [p, wvec.reshape(1, n)]
    if n_rope:
        in_specs += [pl.BlockSpec((ROW_TILE, HEAD_DIM), lambda i: (i, 0))] * 2
        args += [cos, sin]
    out_specs = [pl.BlockSpec((ROW_TILE, n_out), lambda i: (i, 0))]
    out_shape = [jax.ShapeDtypeStruct((r, n_out), BF16)]
    if n_vt:
        out_specs.append(pl.BlockSpec((n_vt, 1, VT_ROWS, ROW_TILE), lambda i: (0, i // per, 0, i % per)))
        out_shape.append(jax.ShapeDtypeStruct((n_vt, r // tk, VT_ROWS, tk), BF16))
    out = pl.pallas_call(
        functools.partial(_prep_kernel, n_norm=n_norm, n_rope=n_rope, n_blocks=n_blocks, n_vt=n_vt),
        grid=(r // ROW_TILE,),
        in_specs=in_specs,
        out_specs=out_specs,
        out_shape=out_shape,
        compiler_params=_cparams(("arbitrary",)),
        name="qkv_prep",
    )(*args)
    return out if n_vt else out[0]


def _attn_kernel(q_ref, k_ref, vt_ref, o_ref, s0_ref, s1_ref, m_ref, acc_ref, *, group, tq, tk, n_lat, n_all):
    n_tiles = n_all // tk
    n_ctx = n_all - n_lat
    m_ref[...] = jnp.full(m_ref.shape, -jnp.inf, F32)
    acc_ref[...] = jnp.zeros(acc_ref.shape, F32)

    def scores(kc, s_ref, g):
        st = _dot_nt(kc, q_ref[:, g * HEAD_DIM:(g + 1) * HEAD_DIM])
        s_ref[g, 0:kc.shape[0], :] = st
        s_ref[g, tk:tk + 1, :] = jnp.max(st, axis=0, keepdims=True)

    def consume(vtc, s_ref, g):
        st = s_ref[g, 0:vtc.shape[1], :]
        m_prev = m_ref[g]
        m_new = jnp.maximum(m_prev, s_ref[g, tk:tk + 1, :])
        p = jnp.exp2(st - m_new).astype(BF16)
        acc_ref[g] = jnp.exp2(m_prev - m_new) * acc_ref[g] + _dot(vtc, p)
        m_ref[g] = m_new

    def key_tile(c):
        return k_ref[pl.ds(pl.multiple_of(c * tk, tk), tk), :]

    def step(k_next, vt_cur, s_next, s_cur):
        for g in range(group):
            if k_next is not None:
                scores(k_next, s_next, g)
            if vt_cur is not None:
                consume(vt_cur, s_cur, g)

    def pair(i, carry):
        c = 2 * i
        step(key_tile(c + 1), vt_ref[0, c], s1_ref, s0_ref)
        step(key_tile(c + 2), vt_ref[0, c + 1], s0_ref, s1_ref)
        return carry

    @pl.when(pl.program_id(1) < n_lat // tq)
    def _():
        step(key_tile(0), None, s0_ref, None)
        lax.fori_loop(0, (n_tiles - 1) // 2, pair, 0)
        step(None, vt_ref[0, n_tiles - 1], None, s0_ref)

    @pl.when(pl.program_id(1) >= n_lat // tq)
    def _():
        step(k_ref[n_lat:n_all, :], None, s0_ref, None)
        step(None, vt_ref[0, n_tiles - 1, :, tk - n_ctx:tk], None, s0_ref)

    for g in range(group):
        out = acc_ref[g, 0:HEAD_DIM, :] / acc_ref[g, HEAD_DIM:HEAD_DIM + 1, :]
        o_ref[:, g * HEAD_DIM:(g + 1) * HEAD_DIM] = out.T.astype(o_ref.dtype)


def _attention(qk, vt, *, n_heads, n_kv, n_lat, q_rows):
    n_all = qk.shape[0]
    group = n_heads // n_kv
    tq = ROW_TILE
    tk = vt.shape[-1]
    gw = group * HEAD_DIM
    assert n_all - n_lat <= tk
    kernel = functools.partial(_attn_kernel, group=group, tq=tq, tk=tk, n_lat=n_lat, n_all=n_all)
    return pl.pallas_call(
        kernel,
        grid=(n_kv, q_rows // tq),
        in_specs=[pl.BlockSpec((tq, gw), lambda h, i: (i, h)),
                  pl.BlockSpec((n_all, HEAD_DIM), lambda h, i: (0, n_heads + h)),
                  pl.BlockSpec((1, n_all // tk, VT_ROWS, tk), lambda h, i: (h, 0, 0, 0))],
        out_specs=pl.BlockSpec((tq, gw), lambda h, i: (i, h)),
        out_shape=jax.ShapeDtypeStruct((q_rows, n_heads * HEAD_DIM), BF16),
        scratch_shapes=[pltpu.VMEM((group, tk + SUBLANES, tq), F32),
                        pltpu.VMEM((group, tk + SUBLANES, tq), F32),
                        pltpu.VMEM((group, 1, tq), F32),
                        pltpu.VMEM((group, VT_ROWS, tq), F32)],
        compiler_params=_cparams(("arbitrary", "arbitrary")),
        name="gqa_attention",
    )(qk, qk, vt)


def _na_kernel(q_ref, k_ref, v_ref, bias_ref, o_ref, *, n_lat, rows, n_ctx):
    nb = rows // NA_QROWS
    b = pl.program_id(1)
    heads = [slice(j * HEAD_DIM, (j + 1) * HEAD_DIM) for j in range(NA_HEADS_PER_STEP)]
    s_c = [_dot_nt(q_ref[:, hs], k_ref[n_lat:n_lat + n_ctx, hs]) for hs in heads]

    @pl.when(b < nb)
    def _():
        w0 = jnp.clip(NA_QROWS * b - NA_ROWS // 2, 0, rows - NA_WROWS)
        win = pl.ds(pl.multiple_of(w0 * GRID_W, GRID_W), NA_WROWS * GRID_W)
        s_w = [_dot_nt(q_ref[:, hs], k_ref[win, hs]) + bias_ref[0, j] for j, hs in enumerate(heads)]
        m = [jnp.maximum(jnp.max(sw, axis=1, keepdims=True), jnp.max(sc, axis=1, keepdims=True))
             for sw, sc in zip(s_w, s_c)]
        p_w = [jnp.exp2(sw - mj) for sw, mj in zip(s_w, m)]
        p_c = [jnp.exp2(sc - mj) for sc, mj in zip(s_c, m)]
        for j, hs in enumerate(heads):
            l = jnp.sum(p_w[j], axis=1, keepdims=True) + jnp.sum(p_c[j], axis=1, keepdims=True)
            o = _dot(p_w[j].astype(BF16), v_ref[win, hs]) + _dot(p_c[j].astype(BF16), v_ref[n_lat:n_lat + n_ctx, hs])
            o_ref[:, hs] = (o / l).astype(o_ref.dtype)

    @pl.when(b >= nb)
    def _():
        for j, hs in enumerate(heads):
            p_c = jnp.exp2(s_c[j] - jnp.max(s_c[j], axis=1, keepdims=True))
            l = jnp.sum(p_c, axis=1, keepdims=True)
            o_ref[:, hs] = (_dot(p_c.astype(BF16), v_ref[n_lat:n_lat + n_ctx, hs]) / l).astype(o_ref.dtype)


def _na_bias_tiles(rpb, rows):
    h = rpb.shape[0]
    col = np.arange(GRID_W)
    c0 = np.clip(col - NA_COLS // 2, 0, GRID_W - NA_COLS)
    col_ok = (col[None, :] >= c0[:, None]) & (col[None, :] < c0[:, None] + NA_COLS)
    pad = GRID_W - NA_COLS
    rp = jnp.pad(rpb * math.log2(math.e), ((0, 0), (0, 0), (pad, pad)))
    toep = jnp.stack([rp[:, :, NA_COLS - 1 - q + pad:NA_COLS - 1 - q + pad + GRID_W] for q in range(GRID_W)], axis=2)
    toep = jnp.where(col_ok[None, None], toep, -jnp.inf)
    neg = jnp.full((h, GRID_W, GRID_W), -jnp.inf, F32)
    half = NA_ROWS // 2
    kinds = (lambda j: (j, 0), lambda j: (half + j, j), lambda j: (NA_ROWS + j, half))
    tiles = []
    for kind in kinds:
        qrows = []
        for j in range(NA_QROWS):
            q_row, win = kind(j)
            blocks = []
            for ik in range(NA_WROWS):
                dr = ik - q_row + (NA_ROWS - 1)
                ok = win <= ik < win + NA_ROWS
                blocks.append(toep[:, dr] if ok else neg)
            qrows.append(jnp.concatenate(blocks, axis=-1))
        tiles.append(jnp.concatenate(qrows, axis=-2))
    return jnp.stack(tiles, axis=0)


def _neighbourhood_attention(qkv, bias, *, n_heads, n_lat, n_ctx):
    r = qkv.shape[0]
    rows = n_lat // GRID_W
    nb = rows // NA_QROWS
    tq = NA_QROWS * GRID_W
    assert tq == n_ctx and rows >= NA_WROWS + NA_QROWS

    def bias_map(h, b):
        return (jnp.where(b == 0, 0, jnp.where(b >= nb - 1, 2, 1)), h, 0, 0)

    kernel = functools.partial(_na_kernel, n_lat=n_lat, rows=rows, n_ctx=n_ctx)
    hps = NA_HEADS_PER_STEP
    hw = hps * HEAD_DIM
    n_steps = n_heads // hps
    return pl.pallas_call(
        kernel,
        grid=(n_steps, r // tq),
        in_specs=[pl.BlockSpec((tq, hw), lambda h, b: (b, h)),
                  pl.BlockSpec((r, hw), lambda h, b: (0, n_steps + h)),
                  pl.BlockSpec((r, hw), lambda h, b: (0, 2 * n_steps + h)),
                  pl.BlockSpec((1, hps, tq, NA_WROWS * GRID_W), bias_map)],
        out_specs=pl.BlockSpec((tq, hw), lambda h, b: (b, h)),
        out_shape=jax.ShapeDtypeStruct((r, n_heads * HEAD_DIM), BF16),
        compiler_params=_cparams(("arbitrary", "arbitrary")),
        name="neighbourhood_attention",
    )(qkv, qkv, qkv, bias)


def _conv_kernel(u_ref, prev_ref, next_ref, w_ref, b_ref, o_ref, ext_ref, *, n_lat):
    tr = u_ref.shape[0]
    i = pl.program_id(0)
    nlb = n_lat // tr
    has_prev = jnp.logical_and(i != 0, i != nlb)
    has_next = i < nlb - 1
    ext_ref[0:SUBLANES, :] = jnp.where(has_prev, prev_ref[...], 0.0)
    ext_ref[SUBLANES:SUBLANES + tr, :] = u_ref[...]
    ext_ref[SUBLANES + tr:, :] = jnp.where(has_next, next_ref[...], 0.0)
    ext = ext_ref[...]
    n_ext = ext.shape[0]
    acc = jnp.broadcast_to(b_ref[...], o_ref.shape)
    for j in range(CONV_K):
        d = j - CONV_K // 2
        shifted = ext if d == 0 else pltpu.roll(ext, (n_ext - d) % n_ext, 0)
        acc = acc + w_ref[j:j + 1, :] * shifted[SUBLANES:SUBLANES + tr, :]
    o_ref[...] = acc * _sigmoid(acc)


def _conv_silu(proj, col0, conv_w, conv_b, n_lat):
    r = proj.shape[0]
    ch = conv_w.shape[1]
    tr = ROW_TILE
    tc = _pick(ch, (1024, 512, 256, 128))
    cb0 = col0 // tc
    assert col0 % tc == 0
    rb = tr // SUBLANES
    last = r // SUBLANES - 1
    return pl.pallas_call(
        functools.partial(_conv_kernel, n_lat=n_lat),
        grid=(r // tr, ch // tc),
        in_specs=[pl.BlockSpec((tr, tc), lambda i, j: (i, cb0 + j)),
                  pl.BlockSpec((SUBLANES, tc), lambda i, j: (jnp.maximum(i * rb - 1, 0), cb0 + j)),
                  pl.BlockSpec((SUBLANES, tc), lambda i, j: (jnp.minimum((i + 1) * rb, last), cb0 + j)),
                  pl.BlockSpec((CONV_K, tc), lambda i, j: (0, j)),
                  pl.BlockSpec((1, tc), lambda i, j: (0, j))],
        out_specs=pl.BlockSpec((tr, tc), lambda i, j: (i, j)),
        out_shape=jax.ShapeDtypeStruct((r, ch), F32),
        scratch_shapes=[pltpu.VMEM((tr + 2 * SUBLANES, tc), F32)],
        compiler_params=_cparams(("arbitrary", "arbitrary")),
        name="conv_silu",
    )(proj, proj, proj, conv_w, conv_b.reshape(1, ch))


def _ssd_kernel(xs_ref, b_ref, c_ref, dt_ref, dtt_ref, bias_ref, biast_ref, alog_ref, alogt_ref, e_ref,
                y_ref, st_ref, *, n_groups, heads_per_group):
    t = xs_ref.shape[0]
    d = pl.program_id(0)
    c = pl.program_id(1)
    gw = heads_per_group * SSM_HEAD_DIM

    @pl.when(c == 0)
    def _():
        st_ref[...] = jnp.zeros(st_ref.shape, F32)

    dt = _softplus(dt_ref[0] + bias_ref[0])
    dtt = _softplus(dtt_ref[0] + biast_ref[0])
    da = dt * (-jnp.exp(alog_ref[0]))
    dat = dtt * (-jnp.exp(alogt_ref[0]))
    row = lax.broadcasted_iota(jnp.int32, (t, t), 0)
    col = lax.broadcasted_iota(jnp.int32, (t, t), 1)
    lag = jnp.where(d == 0, row - col, col - row)
    valid = lag >= 0
    ones_v = jnp.where(valid, 1.0, 0.0).astype(BF16)
    ones_vt = jnp.where(lag <= 0, 1.0, 0.0).astype(BF16)
    cum = sum(_dot(ones_v, piece) for piece in _split3(da))
    cumt = sum(_dot(piece, ones_vt) for piece in _split3(dat))
    tot = jnp.sum(da, axis=0, keepdims=True)
    cumt_in = cumt - jnp.log(dtt)
    stack = jnp.concatenate([dt * jnp.exp(tot - cum), jnp.exp(cum),
                             jnp.broadcast_to(jnp.exp(tot), (SUBLANES, tot.shape[1]))], axis=0)
    wide = _dot(jnp.concatenate(_split3(stack), axis=1), e_ref[...])
    win_w = wide[0:t]
    ecum_w = wide[t:2 * t]
    cdec_w = wide[2 * t:2 * t + 1]
    xs = xs_ref[...]
    xin = (xs * win_w).astype(BF16)
    lane_head = lax.broadcasted_iota(jnp.int32, (t, gw), 1) // SSM_HEAD_DIM

    for g in range(n_groups):
        ns = slice(g * SSM_STATE, (g + 1) * SSM_STATE)
        hs = slice(g * gw, (g + 1) * gw)
        bg = b_ref[:, ns]
        cg = c_ref[:, ns].astype(BF16)
        cb = _dot_nt(cg, bg.astype(BF16))
        acc = _dot(cg, st_ref[g].astype(BF16)) * ecum_w[:, hs]
        xs_g = xs[:, hs].astype(BF16)
        for k in range(heads_per_group):
            h = g * heads_per_group + k
            seg = cum[:, h:h + 1] - cumt_in[h:h + 1, :]
            decay = jnp.exp(jnp.where(valid, seg, -jnp.inf))
            mh = (cb * decay).astype(BF16)
            xk = jnp.where(lane_head == k, xs_g, jnp.zeros_like(xs_g))
            acc = acc + _dot(mh, xk)
        y_ref[0, :, hs] = acc
        st_ref[g] = st_ref[g] * cdec_w[:, hs] + _dot(bg.T.astype(BF16), xin[:, hs])


def _ssd(xbc, dt_raw, dt_bias, a_log, n_lat, n_inner):
    r = xbc.shape[0]
    t = SSD_CHUNK
    heads = n_inner // SSM_HEAD_DIM
    gn = SSM_GROUPS * SSM_STATE
    hpg = heads // SSM_GROUPS
    n_lc = n_lat // t
    assert r - n_lat == t and n_inner % gn == 0
    dt3 = dt_raw[:, :2 * heads].reshape(r, 2, heads).transpose(1, 0, 2)
    dt3t = dt3.transpose(0, 2, 1)
    expand = jnp.asarray(np.tile(np.repeat(np.eye(heads, dtype=np.float32), SSM_HEAD_DIM, axis=1), (3, 1)), BF16)

    def chunk(d, c):
        return jnp.where(c == 0, n_lc, jnp.where(d == 0, c - 1, n_lc - c))

    xb = n_inner // gn
    kernel = functools.partial(_ssd_kernel, n_groups=SSM_GROUPS, heads_per_group=hpg)
    return pl.pallas_call(
        kernel,
        grid=(2, n_lc + 1),
        in_specs=[pl.BlockSpec((t, n_inner), lambda d, c: (chunk(d, c), 0)),
                  pl.BlockSpec((t, gn), lambda d, c: (chunk(d, c), xb)),
                  pl.BlockSpec((t, gn), lambda d, c: (chunk(d, c), xb + 1)),
                  pl.BlockSpec((1, t, heads), lambda d, c: (d, chunk(d, c), 0)),
                  pl.BlockSpec((1, heads, t), lambda d, c: (d, 0, chunk(d, c))),
                  pl.BlockSpec((1, 1, heads), lambda d, c: (d, 0, 0)),
                  pl.BlockSpec((1, heads, 1), lambda d, c: (d, 0, 0)),
                  pl.BlockSpec((1, 1, heads), lambda d, c: (d, 0, 0)),
                  pl.BlockSpec((1, heads, 1), lambda d, c: (d, 0, 0)),
                  pl.BlockSpec((3 * heads, n_inner), lambda d, c: (0, 0))],
        out_specs=pl.BlockSpec((1, t, n_inner), lambda d, c: (d, chunk(d, c), 0)),
        out_shape=jax.ShapeDtypeStruct((2, r, n_inner), F32),
        scratch_shapes=[pltpu.VMEM((SSM_GROUPS, SSM_STATE, hpg * SSM_HEAD_DIM), F32)],
        compiler_params=_cparams(("arbitrary", "arbitrary")),
        name="ssd_scan",
    )(xbc, xbc, xbc, dt3, dt3t, dt_bias.reshape(2, 1, heads), dt_bias.reshape(2, heads, 1),
      a_log.reshape(2, 1, heads), a_log.reshape(2, heads, 1), expand)


def _gated_norm_kernel(y_ref, xs_ref, z_ref, dskip_ref, w_ref, o_ref, *, n_groups):
    z = z_ref[...]
    u = (y_ref[0] + y_ref[1] + dskip_ref[...] * xs_ref[...]) * (z * _sigmoid(z))
    gw = u.shape[1] // n_groups
    for g in range(n_groups):
        sl = slice(g * gw, (g + 1) * gw)
        ug = u[:, sl]
        ug = ug * lax.rsqrt(jnp.mean(ug * ug, axis=-1, keepdims=True) + RMS_EPS)
        o_ref[:, sl] = (ug * w_ref[:, sl]).astype(o_ref.dtype)


def _gated_norm(y, xbc, proj, dskip_w, norm_w):
    _, r, n = y.shape
    return pl.pallas_call(
        functools.partial(_gated_norm_kernel, n_groups=SSM_NORM_GROUPS),
        grid=(r // ROW_TILE,),
        in_specs=[pl.BlockSpec((2, ROW_TILE, n), lambda i: (0, i, 0)),
                  pl.BlockSpec((ROW_TILE, n), lambda i: (i, 0)),
                  pl.BlockSpec((ROW_TILE, n), lambda i: (i, 0)),
                  pl.BlockSpec((1, n), lambda i: (0, 0)),
                  pl.BlockSpec((1, n), lambda i: (0, 0))],
        out_specs=pl.BlockSpec((ROW_TILE, n), lambda i: (i, 0)),
        out_shape=jax.ShapeDtypeStruct((r, n), BF16),
        compiler_params=_cparams(("arbitrary",)),
        name="gated_norm",
    )(y, xbc, proj, dskip_w.reshape(1, n), norm_w.reshape(1, n))


def _rope_tables(n_lat, n_ctx):
    t = jnp.arange(n_lat)
    half = HEAD_DIM // 2
    inv = 1.0 / (ROPE_THETA ** (jnp.arange(0, half, 2, dtype=F32) / half))
    ang = jnp.concatenate([(t // GRID_W).astype(F32)[:, None] * inv,
                           (t % GRID_W).astype(F32)[:, None] * inv], axis=-1)
    cos = jnp.repeat(jnp.cos(ang), 2, axis=-1)
    sin = jnp.repeat(jnp.sin(ang), 2, axis=-1) * jnp.asarray(np.tile(np.array([-1.0, 1.0], np.float32), half))
    cos = jnp.concatenate([cos, jnp.ones((n_ctx, HEAD_DIM), F32)], axis=0)
    sin = jnp.concatenate([sin, jnp.zeros((n_ctx, HEAD_DIM), F32)], axis=0)
    return cos, sin


def kernel(x, c, ctx, c_ctx, w_mod, b_mod, norm1_w, norm2_w, w_mlp_in, w_mlp_out, ab_w_in, ab_conv_w, ab_conv_b,
           ab_dt_bias, ab_a_log, ab_d_skip, ab_norm_w, ab_q_norm, ab_k_norm, ab_rpb, ab_w_out, c_w_qkv, c_q_norm,
           c_k_norm, c_w_out):
    _, n_lat, d = x.shape
    n_ctx = ctx.shape[1]
    depth = w_mod.shape[0]
    assert x.shape[0] == 1 and n_ctx == ROW_TILE and n_lat % 1024 == 0
    n_inner = d
    ssm_heads = n_inner // SSM_HEAD_DIM
    conv_dim = ab_conv_w.shape[-1]
    na_heads = ab_rpb.shape[1]
    attn_heads = c_w_out.shape[1] // HEAD_DIM
    kv_heads = (c_w_qkv.shape[-1] // HEAD_DIM - attn_heads) // 2
    qk_scale = HEAD_DIM ** -0.5 * math.log2(math.e)
    i0 = n_inner
    i1 = i0 + conv_dim
    i2 = i1 + 2 * ssm_heads

    w_mlp_out_b = w_mlp_out.astype(BF16)
    ab_w_dt_b = jnp.pad(ab_w_in[:, :, i1:i2], ((0, 0), (0, 0), (0, LANES - (i2 - i1)))).astype(BF16)
    ab_w_qkv = ab_w_in[:, :, i2:]

    xr = jnp.concatenate([x[0], ctx[0]], axis=0)
    vecs = jnp.zeros((SUBLANES, d), F32).at[0].set(c[0]).at[1].set(c_ctx)
    mods = _modulation(vecs, w_mod, b_mod)[:, :2].reshape(depth, 2, N_MOD, d)
    cos, sin = _rope_tables(n_lat, n_ctx)

    for layer in range(depth):
        last = layer == depth - 1
        i = layer // 2
        mod = mods[layer]
        scsh1 = jnp.stack([mod[:, 1], mod[:, 0]], axis=1)
        scsh2 = jnp.stack([mod[:, 4], mod[:, 3]], axis=1)
        rows = n_lat if last else n_lat + n_ctx
        if layer % 2 == 0:
            h = _normmod(xr, norm1_w[layer], scsh1, n_lat)
            proj = _matmul_w(h, ab_w_in, i, n_cols=i1, name="ab_in_zxbc")
            dt_raw = _matmul(h, ab_w_dt_b, i, name="ab_in_dt")
            qkv = _matmul_w(h, ab_w_qkv, i, name="ab_in_qkv")
            xbc = _conv_silu(proj, i0, ab_conv_w[i], ab_conv_b[i], n_lat)
            y = _ssd(xbc, dt_raw, ab_dt_bias[i], ab_a_log[i], n_lat, n_inner)
            dskip = jnp.repeat(ab_d_skip[i, 0] + ab_d_skip[i, 1], SSM_HEAD_DIM)
            gn = _gated_norm(y, xbc, proj, dskip, ab_norm_w[i])
            wvec = jnp.concatenate([jnp.tile(ab_q_norm[i] * qk_scale, na_heads), jnp.tile(ab_k_norm[i], na_heads),
                                    jnp.ones((na_heads * HEAD_DIM,), F32)])
            qkvb = _prep(qkv, wvec, 2 * na_heads)
            bias = _na_bias_tiles(ab_rpb[i], n_lat // GRID_W)
            al = _neighbourhood_attention(qkvb, bias, n_heads=na_heads, n_lat=n_lat, n_ctx=n_ctx)
            xr = _matmul_w(gn, ab_w_out, i, x2=al, rows=rows, tn=512, epilogue="resid", res=xr, gate=mod[:, 2, :],
                           n_lat=n_lat, name="mixer_out")
        else:
            p = _matmul_w(xr, c_w_qkv, i, norm=(norm1_w[layer], scsh1), n_lat=n_lat, name="c_in_qkv")
            wvec = jnp.concatenate([jnp.tile(c_q_norm[i] * qk_scale, attn_heads), jnp.tile(c_k_norm[i], kv_heads),
                                    jnp.ones((kv_heads * HEAD_DIM,), F32)])
            qk, vt = _prep(p, wvec, attn_heads + kv_heads, attn_heads + kv_heads, cos, sin, n_vt=kv_heads)
            att = _attention(qk, vt, n_heads=attn_heads, n_kv=kv_heads, n_lat=n_lat, q_rows=rows)
            xr = _matmul_w(att, c_w_out, i, rows=rows, epilogue="resid", res=xr, gate=mod[:, 2, :], n_lat=n_lat,
                           name="mixer_out")
        a = _matmul_w(xr, w_mlp_in, layer, norm=(norm2_w[layer], scsh2), out_dtype=BF16, epilogue="relu2",
                      n_lat=n_lat, name="mlp_in")
        xr = _matmul(a, w_mlp_out_b, layer, epilogue="resid", res=xr, gate=mod[:, 5, :], n_lat=n_lat,
                     name="mlp_out")
    return xr[None]
```

```python
import functools
import math

import numpy as np
import jax
import jax.numpy as jnp
from jax import lax
from jax.experimental import pallas as pl
from jax.experimental.pallas import tpu as pltpu

F32 = jnp.float32
BF16 = jnp.bfloat16

GRID_W = 64
HEAD_DIM = 128
ROPE_THETA = 10000.0
RMS_EPS = 1e-6
N_MOD = 6
SSM_HEAD_DIM = 64
SSM_GROUPS = 8
SSM_STATE = 128
SSM_NORM_GROUPS = 8
CONV_K = 5
NA_ROWS = 8
NA_COLS = 16

LANES = 128
SUBLANES = 8
VMEM_LIMIT_BYTES = 48 * 1024 * 1024
VMEM_LIMIT_WEIGHT_RESIDENT = 56 * 1024 * 1024

ROW_TILE = 256
SSD_CHUNK = 256
VT_ROWS = HEAD_DIM + 2 * SUBLANES
NA_QROWS = 4
NA_WROWS = 12
NA_HEADS_PER_STEP = 2
ATTN_Q_TILE = 512


def _pick(n, candidates):
    for c in candidates:
        if n % c == 0:
            return c
    raise ValueError(f"no tile in {candidates} divides {n}")


def _cparams(sem):
    return pltpu.CompilerParams(dimension_semantics=sem, vmem_limit_bytes=VMEM_LIMIT_BYTES)


def _sigmoid(v):
    return 1.0 / (1.0 + jnp.exp(-v))


def _softplus(v):
    return jnp.maximum(v, 0.0) + jnp.log(1.0 + jnp.exp(-jnp.abs(v)))


def _dot(a, b):
    return jnp.dot(a, b, preferred_element_type=F32)


def _dot_nt(a, b):
    return lax.dot_general(a, b, (((1,), (1,)), ((), ())), preferred_element_type=F32)


def _split3(x):
    hi = x.astype(BF16)
    r1 = x - hi.astype(F32)
    mid = r1.astype(BF16)
    lo = (r1 - mid.astype(F32)).astype(BF16)
    return hi, mid, lo


def _mod_kernel(v_ref, w_ref, b_ref, o_ref):
    v = v_ref[...]
    s = (v * _sigmoid(v)).astype(BF16)
    o_ref[0] = _dot(s, w_ref[0].astype(BF16)) + b_ref[0]


def _modulation(vecs, w_mod, b_mod):
    depth, d, n = w_mod.shape
    tn = _pick(n, (1024, 512, 256, 128))
    return pl.pallas_call(
        _mod_kernel,
        grid=(depth, n // tn),
        in_specs=[pl.BlockSpec((SUBLANES, d), lambda l, j: (0, 0)),
                  pl.BlockSpec((1, d, tn), lambda l, j: (l, 0, j)),
                  pl.BlockSpec((1, 1, tn), lambda l, j: (l, 0, j))],
        out_specs=pl.BlockSpec((1, SUBLANES, tn), lambda l, j: (l, 0, j)),
        out_shape=jax.ShapeDtypeStruct((depth, SUBLANES, n), F32),
        compiler_params=_cparams(("arbitrary", "arbitrary")),
        name="modulation",
    )(vecs, w_mod, b_mod.reshape(depth, 1, n))


def _normmod_kernel(x_ref, w_ref, m_ref, o_ref):
    x = x_ref[...]
    y = x * lax.rsqrt(jnp.mean(x * x, axis=-1, keepdims=True) + RMS_EPS)
    m = m_ref[0]
    o_ref[...] = ((y * w_ref[...]) * (1.0 + m[0:1, :]) + m[1:2, :]).astype(o_ref.dtype)


def _normmod(x, w, scsh, n_lat):
    r, d = x.shape
    nlb = n_lat // ROW_TILE
    return pl.pallas_call(
        _normmod_kernel,
        grid=(r // ROW_TILE,),
        in_specs=[pl.BlockSpec((ROW_TILE, d), lambda i: (i, 0)),
                  pl.BlockSpec((1, d), lambda i: (0, 0)),
                  pl.BlockSpec((1, 2, d), lambda i: (i // nlb, 0, 0))],
        out_specs=pl.BlockSpec((ROW_TILE, d), lambda i: (i, 0)),
        out_shape=jax.ShapeDtypeStruct((r, d), BF16),
        compiler_params=_cparams(("arbitrary",)),
        name="normmod",
    )(x, w.reshape(1, d), scsh)


def _mm_kernel(*refs, nk, nk1, two_lhs, epilogue, n_lat, tm):
    refs = list(refs)
    x_ref = refs.pop(0)
    x2_ref = refs.pop(0) if two_lhs else None
    w_ref = refs.pop(0)
    if epilogue == "resid":
        res_ref = refs.pop(0)
        gate_ref = refs.pop(0)
    o_ref = refs.pop(0)

    def finish(acc):
        if epilogue == "relu2":
            a = jnp.maximum(acc, 0.0)
            o_ref[...] = (a * a).astype(o_ref.dtype)
        elif epilogue == "resid":
            row = pl.program_id(0) * tm + lax.broadcasted_iota(jnp.int32, (tm, 1), 0)
            g = jnp.where(row < n_lat, gate_ref[0:1, :], gate_ref[1:2, :])
            o_ref[...] = res_ref[...] + g * acc
        else:
            o_ref[...] = acc.astype(o_ref.dtype)

    if nk == 1:
        finish(_dot(x_ref[...], w_ref[0]))
        return
    acc_ref, = refs
    k = pl.program_id(2)

    @pl.when(k == 0)
    def _():
        acc_ref[...] = jnp.zeros_like(acc_ref)

    if two_lhs:
        @pl.when(k < nk1)
        def _():
            acc_ref[...] += _dot(x_ref[...], w_ref[0])

        @pl.when(k >= nk1)
        def _():
            acc_ref[...] += _dot(x2_ref[...], w_ref[0])
    else:
        acc_ref[...] += _dot(x_ref[...], w_ref[0])

    @pl.when(k == nk - 1)
    def _():
        finish(acc_ref[...])


def _matmul(x, w, layer, *, x2=None, rows=None, n_cols=None, out_dtype=F32, epilogue="none", res=None, gate=None,
            n_lat=0, name="matmul"):
    r = x.shape[0] if rows is None else rows
    _, kdim, n_w = w.shape
    n = n_w if n_cols is None else n_cols
    k1 = x.shape[1]
    tm = _pick(r, (1024, 768, 512, 256))
    tn = _pick(n, (1024, 512, 256, 128))
    if x2 is None:
        tk = kdim if kdim <= 2048 else _pick(kdim, (2048, 1024, 512))
    else:
        tk = math.gcd(k1, x2.shape[1])
    nk, nk1 = kdim // tk, k1 // tk
    in_specs = [pl.BlockSpec((tm, tk), lambda i, j, k: (i, jnp.minimum(k, nk1 - 1)))]
    args = [x]
    if x2 is not None:
        in_specs.append(pl.BlockSpec((tm, tk), lambda i, j, k: (i, jnp.maximum(k - nk1, 0))))
        args.append(x2)
    in_specs.append(pl.BlockSpec((1, tk, tn), lambda i, j, k: (layer, k, j)))
    args.append(w)
    if epilogue == "resid":
        in_specs += [pl.BlockSpec((tm, tn), lambda i, j, k: (i, j)),
                     pl.BlockSpec((2, tn), lambda i, j, k: (0, j))]
        args += [res, gate]
    kernel = functools.partial(_mm_kernel, nk=nk, nk1=nk1, two_lhs=x2 is not None, epilogue=epilogue, n_lat=n_lat,
                               tm=tm)
    return pl.pallas_call(
        kernel,
        grid=(r // tm, n // tn, nk),
        in_specs=in_specs,
        out_specs=pl.BlockSpec((tm, tn), lambda i, j, k: (i, j)),
        out_shape=jax.ShapeDtypeStruct((r, n), out_dtype),
        scratch_shapes=[] if nk == 1 else [pltpu.VMEM((tm, tn), F32)],
        compiler_params=_cparams(("arbitrary", "arbitrary", "arbitrary")),
        name=name,
    )(*args)


def _mmw_kernel(*refs, prologue, two_lhs, epilogue, n_lat, tm, k1, qk_tiles):
    refs = list(refs)
    x_ref = refs.pop(0)
    if prologue:
        nw_ref = refs.pop(0)
        mod_ref = refs.pop(0)
    x2_ref = refs.pop(0) if two_lhs else None
    w_ref = refs.pop(0)
    if epilogue == "resid":
        res_ref = refs.pop(0)
        gate_ref = refs.pop(0)
    if epilogue == "qknorm":
        qkw_ref = refs.pop(0)
    o_ref, wb_ref = refs[:2]
    j = pl.program_id(0)
    i = pl.program_id(1)

    @pl.when(i == 0)
    def _():
        wb_ref[...] = w_ref[0].astype(BF16)

    if prologue:
        h_ref = refs[2]
        for s in range(tm // ROW_TILE):
            rs = slice(s * ROW_TILE, (s + 1) * ROW_TILE)
            ctx_rows = i * tm + s * ROW_TILE >= n_lat
            scale = nw_ref[...] * (1.0 + jnp.where(ctx_rows, mod_ref[1, 0:1, :], mod_ref[0, 0:1, :]))
            shift = jnp.where(ctx_rows, mod_ref[1, 1:2, :], mod_ref[0, 1:2, :])
            x = x_ref[rs, :]
            y = x * lax.rsqrt(jnp.mean(x * x, axis=-1, keepdims=True) + RMS_EPS)
            h_ref[rs, :] = (y * scale + shift).astype(BF16)
        h = h_ref[...]
    else:
        h = x_ref[...]
    if two_lhs:
        acc = _dot(h, wb_ref[0:k1, :]) + _dot(x2_ref[...], wb_ref[k1:, :])
    else:
        acc = _dot(h, wb_ref[...])
    if epilogue == "relu2":
        a = jnp.maximum(acc, 0.0)
        o_ref[...] = (a * a).astype(o_ref.dtype)
    elif epilogue == "resid":
        is_lat = (i * tm + lax.broadcasted_iota(jnp.int32, (tm, 1), 0)) < n_lat
        o_ref[...] = res_ref[...] + jnp.where(is_lat, gate_ref[0:1, :], gate_ref[1:2, :]) * acc
    elif epilogue == "qknorm":
        @pl.when(j < qk_tiles)
        def _():
            for b in range(acc.shape[1] // HEAD_DIM):
                sl = slice(b * HEAD_DIM, (b + 1) * HEAD_DIM)
                u = acc[:, sl]
                u = u * lax.rsqrt(jnp.mean(u * u, axis=-1, keepdims=True) + RMS_EPS) * qkw_ref[:, sl]
                o_ref[:, sl] = u.astype(o_ref.dtype)

        @pl.when(j >= qk_tiles)
        def _():
            o_ref[...] = acc.astype(o_ref.dtype)
    else:
        o_ref[...] = acc.astype(o_ref.dtype)


def _matmul_w(x, w, layer, *, norm=None, x2=None, rows=None, n_cols=None, tn=1024, out_dtype=F32, epilogue="none",
              res=None, gate=None, qk_w=None, n_lat=0, name="matmul_w"):
    r = x.shape[0] if rows is None else rows
    _, kdim, n_w = w.shape
    n = n_w if n_cols is None else n_cols
    k1 = x.shape[1]
    tm = _pick(r, (1024, 768, 512, 256))
    in_specs = [pl.BlockSpec((tm, k1), lambda j, i: (i, 0))]
    args = [x]
    scratch = [pltpu.VMEM((kdim, tn), BF16)]
    if norm is not None:
        norm_w, scsh = norm
        in_specs += [pl.BlockSpec((1, k1), lambda j, i: (0, 0)),
                     pl.BlockSpec((2, 2, k1), lambda j, i: (0, 0, 0))]
        args += [norm_w.reshape(1, k1), scsh]
        scratch.append(pltpu.VMEM((tm, k1), BF16))
    if x2 is not None:
        in_specs.append(pl.BlockSpec((tm, x2.shape[1]), lambda j, i: (i, 0)))
        args.append(x2)
    in_specs.append(pl.BlockSpec((1, kdim, tn), lambda j, i: (layer, 0, j)))
    args.append(w)
    if epilogue == "resid":
        in_specs += [pl.BlockSpec((tm, tn), lambda j, i: (i, j)),
                     pl.BlockSpec((2, tn), lambda j, i: (0, j))]
        args += [res, gate]
    qk_tiles = 0
    if epilogue == "qknorm":
        qk_tiles = qk_w.shape[0] // tn
        assert qk_w.shape[0] % tn == 0
        in_specs.append(pl.BlockSpec((1, tn), lambda j, i: (0, jnp.minimum(j, qk_tiles - 1))))
        args.append(qk_w.reshape(1, -1))
    kernel = functools.partial(_mmw_kernel, prologue=norm is not None, two_lhs=x2 is not None, epilogue=epilogue,
                               n_lat=n_lat, tm=tm, k1=k1, qk_tiles=qk_tiles)
    return pl.pallas_call(
        kernel,
        grid=(n // tn, r // tm),
        in_specs=in_specs,
        out_specs=pl.BlockSpec((tm, tn), lambda j, i: (i, j)),
        out_shape=jax.ShapeDtypeStruct((r, n), out_dtype),
        scratch_shapes=scratch,
        compiler_params=pltpu.CompilerParams(dimension_semantics=("arbitrary", "arbitrary"),
                                             vmem_limit_bytes=VMEM_LIMIT_WEIGHT_RESIDENT),
        name=name,
    )(*args)


def _prep_kernel(*refs, n_norm, n_rope, n_blocks, n_vt):
    refs = list(refs)
    p_ref, w_ref = refs[:2]
    if n_rope:
        cos = refs[2][...]
        sin = refs[3][...]
        even = (lax.broadcasted_iota(jnp.int32, cos.shape, 1) % 2) == 0
    o_ref = refs[-2] if n_vt else refs[-1]
    for b in range(n_blocks):
        sl = slice(b * HEAD_DIM, (b + 1) * HEAD_DIM)
        u = p_ref[:, sl]
        if b < n_norm:
            u = u * lax.rsqrt(jnp.mean(u * u, axis=-1, keepdims=True) + RMS_EPS) * w_ref[:, sl]
        if b < n_rope:
            partner = jnp.where(even, pltpu.roll(u, HEAD_DIM - 1, 1), pltpu.roll(u, 1, 1))
            u = u * cos + partner * sin
        if b < n_blocks - n_vt:
            o_ref[:, sl] = u.astype(o_ref.dtype)
        else:
            vt_ref = refs[-1]
            hv = b - (n_blocks - n_vt)
            vt_ref[hv, 0, 0:HEAD_DIM, :] = u.T.astype(BF16)
            pad_row = lax.broadcasted_iota(jnp.int32, (VT_ROWS - HEAD_DIM, u.shape[0]), 0)
            vt_ref[hv, 0, HEAD_DIM:, :] = jnp.where(pad_row == 0, 1.0, 0.0).astype(BF16)


def _attn_key_tile(n_all):
    for tk in (768, ROW_TILE):
        if n_all % tk == 0 and (n_all // tk) % 2 == 1:
            return tk
    raise ValueError(f"no key tile for {n_all} rows")


def _prep(p, wvec, n_norm, n_rope=0, cos=None, sin=None, n_vt=0):
    r, n = p.shape
    n_blocks = n // HEAD_DIM
    n_out = (n_blocks - n_vt) * HEAD_DIM
    tk = _attn_key_tile(r) if n_vt else ROW_TILE
    per = tk // ROW_TILE
    in_specs = [pl.BlockSpec((ROW_TILE, n), lambda i: (i, 0)),
                pl.BlockSpec((1, n), lambda i: (0, 0))]
    args = [p, wvec.reshape(1, n)]
    if n_rope:
        in_specs += [pl.BlockSpec((ROW_TILE, HEAD_DIM), lambda i: (i, 0))] * 2
        args += [cos, sin]
    out_specs = [pl.BlockSpec((ROW_TILE, n_out), lambda i: (i, 0))]
    out_shape = [jax.ShapeDtypeStruct((r, n_out), BF16)]
    if n_vt:
        out_specs.append(pl.BlockSpec((n_vt, 1, VT_ROWS, ROW_TILE), lambda i: (0, i // per, 0, i % per)))
        out_shape.append(jax.ShapeDtypeStruct((n_vt, r // tk, VT_ROWS, tk), BF16))
    out = pl.pallas_call(
        functools.partial(_prep_kernel, n_norm=n_norm, n_rope=n_rope, n_blocks=n_blocks, n_vt=n_vt),
        grid=(r // ROW_TILE,),
        in_specs=in_specs,
        out_specs=out_specs,
        out_shape=out_shape,
        compiler_params=_cparams(("arbitrary",)),
        name="qkv_prep",
    )(*args)
    return out if n_vt else out[0]


def _attn_kernel(q_ref, k_ref, vt_ref, o_ref, s0_ref, s1_ref, m_ref, acc_ref, *, group, tq, tk, n_lat, n_all):
    n_tiles = n_all // tk
    n_ctx = n_all - n_lat
    m_ref[...] = jnp.full(m_ref.shape, -jnp.inf, F32)
    acc_ref[...] = jnp.zeros(acc_ref.shape, F32)

    def scores(kc, s_ref, g):
        st = _dot_nt(kc, q_ref[:, g * HEAD_DIM:(g + 1) * HEAD_DIM])
        s_ref[g, 0:kc.shape[0], :] = st
        s_ref[g, tk:tk + 1, :] = jnp.max(st, axis=0, keepdims=True)

    def consume(vtc, s_ref, g):
        st = s_ref[g, 0:vtc.shape[1], :]
        m_prev = m_ref[g]
        m_new = jnp.maximum(m_prev, s_ref[g, tk:tk + 1, :])
        p = jnp.exp2(st - m_new).astype(BF16)
        acc_ref[g] = jnp.exp2(m_prev - m_new) * acc_ref[g] + _dot(vtc, p)
        m_ref[g] = m_new

    def key_tile(c):
        return k_ref[pl.ds(pl.multiple_of(c * tk, tk), tk), :]

    def step(k_next, vt_cur, s_next, s_cur):
        for g in range(group):
            if k_next is not None:
                scores(k_next, s_next, g)
            if vt_cur is not None:
                consume(vt_cur, s_cur, g)

    def pair(i, carry):
        c = 2 * i
        step(key_tile(c + 1), vt_ref[0, c], s1_ref, s0_ref)
        step(key_tile(c + 2), vt_ref[0, c + 1], s0_ref, s1_ref)
        return carry

    @pl.when(pl.program_id(1) < n_lat // tq)
    def _():
        step(key_tile(0), None, s0_ref, None)
        lax.fori_loop(0, (n_tiles - 1) // 2, pair, 0)
        step(None, vt_ref[0, n_tiles - 1], None, s0_ref)

    @pl.when(pl.program_id(1) >= n_lat // tq)
    def _():
        step(k_ref[n_lat:n_all, :], None, s0_ref, None)
        step(None, vt_ref[0, n_tiles - 1, :, tk - n_ctx:tk], None, s0_ref)

    for g in range(group):
        out = acc_ref[g, 0:HEAD_DIM, :] / acc_ref[g, HEAD_DIM:HEAD_DIM + 1, :]
        o_ref[:, g * HEAD_DIM:(g + 1) * HEAD_DIM] = out.T.astype(o_ref.dtype)


def _attention(qk, vt, *, n_heads, n_kv, n_lat, q_rows):
    n_all = qk.shape[0]
    group = n_heads // n_kv
    tq = ATTN_Q_TILE
    tk = vt.shape[-1]
    gw = group * HEAD_DIM
    assert n_all - n_lat <= min(tk, tq) and n_lat % tq == 0
    kernel = functools.partial(_attn_kernel, group=group, tq=tq, tk=tk, n_lat=n_lat, n_all=n_all)
    return pl.pallas_call(
        kernel,
        grid=(n_kv, pl.cdiv(q_rows, tq)),
        in_specs=[pl.BlockSpec((tq, gw), lambda h, i: (i, h)),
                  pl.BlockSpec((n_all, HEAD_DIM), lambda h, i: (0, n_heads + h)),
                  pl.BlockSpec((1, n_all // tk, VT_ROWS, tk), lambda h, i: (h, 0, 0, 0))],
        out_specs=pl.BlockSpec((tq, gw), lambda h, i: (i, h)),
        out_shape=jax.ShapeDtypeStruct((q_rows, n_heads * HEAD_DIM), BF16),
        scratch_shapes=[pltpu.VMEM((group, tk + SUBLANES, tq), F32),
                        pltpu.VMEM((group, tk + SUBLANES, tq), F32),
                        pltpu.VMEM((group, 1, tq), F32),
                        pltpu.VMEM((group, VT_ROWS, tq), F32)],
        compiler_params=_cparams(("arbitrary", "arbitrary")),
        name="gqa_attention",
    )(qk, qk, vt)


def _na_kernel(q_ref, k_ref, v_ref, bias_ref, o_ref, *, n_lat, rows, n_ctx):
    nb = rows // NA_QROWS
    b = pl.program_id(1)
    heads = [slice(j * HEAD_DIM, (j + 1) * HEAD_DIM) for j in range(NA_HEADS_PER_STEP)]
    s_c = [_dot_nt(q_ref[:, hs], k_ref[n_lat:n_lat + n_ctx, hs]) for hs in heads]

    @pl.when(b < nb)
    def _():
        w0 = jnp.clip(NA_QROWS * b - NA_ROWS // 2, 0, rows - NA_WROWS)
        win = pl.ds(pl.multiple_of(w0 * GRID_W, GRID_W), NA_WROWS * GRID_W)
        s_w = [_dot_nt(q_ref[:, hs], k_ref[win, hs]) + bias_ref[0, j] for j, hs in enumerate(heads)]
        m = [jnp.maximum(jnp.max(sw, axis=1, keepdims=True), jnp.max(sc, axis=1, keepdims=True))
             for sw, sc in zip(s_w, s_c)]
        p_w = [jnp.exp2(sw - mj) for sw, mj in zip(s_w, m)]
        p_c = [jnp.exp2(sc - mj) for sc, mj in zip(s_c, m)]
        for j, hs in enumerate(heads):
            l = jnp.sum(p_w[j], axis=1, keepdims=True) + jnp.sum(p_c[j], axis=1, keepdims=True)
            o = _dot(p_w[j].astype(BF16), v_ref[win, hs]) + _dot(p_c[j].astype(BF16), v_ref[n_lat:n_lat + n_ctx, hs])
            o_ref[:, hs] = (o / l).astype(o_ref.dtype)

    @pl.when(b >= nb)
    def _():
        for j, hs in enumerate(heads):
            p_c = jnp.exp2(s_c[j] - jnp.max(s_c[j], axis=1, keepdims=True))
            l = jnp.sum(p_c, axis=1, keepdims=True)
            o_ref[:, hs] = (_dot(p_c.astype(BF16), v_ref[n_lat:n_lat + n_ctx, hs]) / l).astype(o_ref.dtype)


def _na_bias_tiles(rpb, rows):
    h = rpb.shape[0]
    col = np.arange(GRID_W)
    c0 = np.clip(col - NA_COLS // 2, 0, GRID_W - NA_COLS)
    col_ok = (col[None, :] >= c0[:, None]) & (col[None, :] < c0[:, None] + NA_COLS)
    pad = GRID_W - NA_COLS
    rp = jnp.pad(rpb * math.log2(math.e), ((0, 0), (0, 0), (pad, pad)))
    toep = jnp.stack([rp[:, :, NA_COLS - 1 - q + pad:NA_COLS - 1 - q + pad + GRID_W] for q in range(GRID_W)], axis=2)
    toep = jnp.where(col_ok[None, None], toep, -jnp.inf)
    neg = jnp.full((h, GRID_W, GRID_W), -jnp.inf, F32)
    half = NA_ROWS // 2
    kinds = (lambda j: (j, 0), lambda j: (half + j, j), lambda j: (NA_ROWS + j, half))
    tiles = []
    for kind in kinds:
        qrows = []
        for j in range(NA_QROWS):
            q_row, win = kind(j)
            blocks = []
            for ik in range(NA_WROWS):
                dr = ik - q_row + (NA_ROWS - 1)
                ok = win <= ik < win + NA_ROWS
                blocks.append(toep[:, dr] if ok else neg)
            qrows.append(jnp.concatenate(blocks, axis=-1))
        tiles.append(jnp.concatenate(qrows, axis=-2))
    return jnp.stack(tiles, axis=0)


def _neighbourhood_attention(qkv, bias, *, n_heads, n_lat, n_ctx):
    r = qkv.shape[0]
    rows = n_lat // GRID_W
    nb = rows // NA_QROWS
    tq = NA_QROWS * GRID_W
    assert tq == n_ctx and rows >= NA_WROWS + NA_QROWS

    def bias_map(h, b):
        return (jnp.where(b == 0, 0, jnp.where(b >= nb - 1, 2, 1)), h, 0, 0)

    kernel = functools.partial(_na_kernel, n_lat=n_lat, rows=rows, n_ctx=n_ctx)
    hps = NA_HEADS_PER_STEP
    hw = hps * HEAD_DIM
    n_steps = n_heads // hps
    return pl.pallas_call(
        kernel,
        grid=(n_steps, r // tq),
        in_specs=[pl.BlockSpec((tq, hw), lambda h, b: (b, h)),
                  pl.BlockSpec((r, hw), lambda h, b: (0, n_steps + h)),
                  pl.BlockSpec((r, hw), lambda h, b: (0, 2 * n_steps + h)),
                  pl.BlockSpec((1, hps, tq, NA_WROWS * GRID_W), bias_map)],
        out_specs=pl.BlockSpec((tq, hw), lambda h, b: (b, h)),
        out_shape=jax.ShapeDtypeStruct((r, n_heads * HEAD_DIM), BF16),
        compiler_params=_cparams(("arbitrary", "arbitrary")),
        name="neighbourhood_attention",
    )(qkv, qkv, qkv, bias)


def _conv_kernel(u_ref, prev_ref, next_ref, w_ref, b_ref, o_ref, ext_ref, *, n_lat):
    tr = u_ref.shape[0]
    i = pl.program_id(0)
    nlb = n_lat // tr
    has_prev = jnp.logical_and(i != 0, i != nlb)
    has_next = i < nlb - 1
    ext_ref[0:SUBLANES, :] = jnp.where(has_prev, prev_ref[...], 0.0)
    ext_ref[SUBLANES:SUBLANES + tr, :] = u_ref[...]
    ext_ref[SUBLANES + tr:, :] = jnp.where(has_next, next_ref[...], 0.0)
    ext = ext_ref[...]
    n_ext = ext.shape[0]
    acc = jnp.broadcast_to(b_ref[...], o_ref.shape)
    for j in range(CONV_K):
        d = j - CONV_K // 2
        shifted = ext if d == 0 else pltpu.roll(ext, (n_ext - d) % n_ext, 0)
        acc = acc + w_ref[j:j + 1, :] * shifted[SUBLANES:SUBLANES + tr, :]
    o_ref[...] = acc * _sigmoid(acc)


def _conv_silu(proj, col0, conv_w, conv_b, n_lat):
    r = proj.shape[0]
    ch = conv_w.shape[1]
    tr = ROW_TILE
    tc = _pick(ch, (1024, 512, 256, 128))
    cb0 = col0 // tc
    assert col0 % tc == 0
    rb = tr // SUBLANES
    last = r // SUBLANES - 1
    return pl.pallas_call(
        functools.partial(_conv_kernel, n_lat=n_lat),
        grid=(r // tr, ch // tc),
        in_specs=[pl.BlockSpec((tr, tc), lambda i, j: (i, cb0 + j)),
                  pl.BlockSpec((SUBLANES, tc), lambda i, j: (jnp.maximum(i * rb - 1, 0), cb0 + j)),
                  pl.BlockSpec((SUBLANES, tc), lambda i, j: (jnp.minimum((i + 1) * rb, last), cb0 + j)),
                  pl.BlockSpec((CONV_K, tc), lambda i, j: (0, j)),
                  pl.BlockSpec((1, tc), lambda i, j: (0, j))],
        out_specs=pl.BlockSpec((tr, tc), lambda i, j: (i, j)),
        out_shape=jax.ShapeDtypeStruct((r, ch), F32),
        scratch_shapes=[pltpu.VMEM((tr + 2 * SUBLANES, tc), F32)],
        compiler_params=_cparams(("arbitrary", "arbitrary")),
        name="conv_silu",
    )(proj, proj, proj, conv_w, conv_b.reshape(1, ch))


def _ssd_kernel(xs_ref, b_ref, c_ref, dt_ref, dtt_ref, bias_ref, biast_ref, alog_ref, alogt_ref, e_ref,
                y_ref, st_ref, *, n_groups, heads_per_group):
    t = xs_ref.shape[0]
    d = pl.program_id(0)
    c = pl.program_id(1)
    gw = heads_per_group * SSM_HEAD_DIM

    @pl.when(c == 0)
    def _():
        st_ref[...] = jnp.zeros(st_ref.shape, F32)

    dt = _softplus(dt_ref[0] + bias_ref[0])
    dtt = _softplus(dtt_ref[0] + biast_ref[0])
    da = dt * (-jnp.exp(alog_ref[0]))
    dat = dtt * (-jnp.exp(alogt_ref[0]))
    row = lax.broadcasted_iota(jnp.int32, (t, t), 0)
    col = lax.broadcasted_iota(jnp.int32, (t, t), 1)
    lag = jnp.where(d == 0, row - col, col - row)
    valid = lag >= 0
    ones_v = jnp.where(valid, 1.0, 0.0).astype(BF16)
    ones_vt = jnp.where(lag <= 0, 1.0, 0.0).astype(BF16)
    cum = sum(_dot(ones_v, piece) for piece in _split3(da))
    cumt = sum(_dot(piece, ones_vt) for piece in _split3(dat))
    tot = jnp.sum(da, axis=0, keepdims=True)
    cumt_in = cumt - jnp.log(dtt)
    stack = jnp.concatenate([dt * jnp.exp(tot - cum), jnp.exp(cum),
                             jnp.broadcast_to(jnp.exp(tot), (SUBLANES, tot.shape[1]))], axis=0)
    wide = _dot(jnp.concatenate(_split3(stack), axis=1), e_ref[...])
    win_w = wide[0:t]
    ecum_w = wide[t:2 * t]
    cdec_w = wide[2 * t:2 * t + 1]
    xs = xs_ref[...]
    xin = (xs * win_w).astype(BF16)
    lane_head = lax.broadcasted_iota(jnp.int32, (t, gw), 1) // SSM_HEAD_DIM

    for g in range(n_groups):
        ns = slice(g * SSM_STATE, (g + 1) * SSM_STATE)
        hs = slice(g * gw, (g + 1) * gw)
        bg = b_ref[:, ns]
        cg = c_ref[:, ns].astype(BF16)
        cb = _dot_nt(cg, bg.astype(BF16))
        acc = _dot(cg, st_ref[g].astype(BF16)) * ecum_w[:, hs]
        xs_g = xs[:, hs].astype(BF16)
        for k in range(heads_per_group):
            h = g * heads_per_group + k
            seg = cum[:, h:h + 1] - cumt_in[h:h + 1, :]
            decay = jnp.exp(jnp.where(valid, seg, -jnp.inf))
            mh = (cb * decay).astype(BF16)
            xk = jnp.where(lane_head == k, xs_g, jnp.zeros_like(xs_g))
            acc = acc + _dot(mh, xk)
        y_ref[0, :, hs] = acc
        st_ref[g] = st_ref[g] * cdec_w[:, hs] + _dot(bg.T.astype(BF16), xin[:, hs])


def _ssd(xbc, dt_raw, dt_bias, a_log, n_lat, n_inner):
    r = xbc.shape[0]
    t = SSD_CHUNK
    heads = n_inner // SSM_HEAD_DIM
    gn = SSM_GROUPS * SSM_STATE
    hpg = heads // SSM_GROUPS
    n_lc = n_lat // t
    assert r - n_lat == t and n_inner % gn == 0
    dt3 = dt_raw[:, :2 * heads].reshape(r, 2, heads).transpose(1, 0, 2)
    dt3t = dt3.transpose(0, 2, 1)
    expand = jnp.asarray(np.tile(np.repeat(np.eye(heads, dtype=np.float32), SSM_HEAD_DIM, axis=1), (3, 1)), BF16)

    def chunk(d, c):
        return jnp.where(c == 0, n_lc, jnp.where(d == 0, c - 1, n_lc - c))

    xb = n_inner // gn
    kernel = functools.partial(_ssd_kernel, n_groups=SSM_GROUPS, heads_per_group=hpg)
    return pl.pallas_call(
        kernel,
        grid=(2, n_lc + 1),
        in_specs=[pl.BlockSpec((t, n_inner), lambda d, c: (chunk(d, c), 0)),
                  pl.BlockSpec((t, gn), lambda d, c: (chunk(d, c), xb)),
                  pl.BlockSpec((t, gn), lambda d, c: (chunk(d, c), xb + 1)),
                  pl.BlockSpec((1, t, heads), lambda d, c: (d, chunk(d, c), 0)),
                  pl.BlockSpec((1, heads, t), lambda d, c: (d, 0, chunk(d, c))),
                  pl.BlockSpec((1, 1, heads), lambda d, c: (d, 0, 0)),
                  pl.BlockSpec((1, heads, 1), lambda d, c: (d, 0, 0)),
                  pl.BlockSpec((1, 1, heads), lambda d, c: (d, 0, 0)),
                  pl.BlockSpec((1, heads, 1), lambda d, c: (d, 0, 0)),
                  pl.BlockSpec((3 * heads, n_inner), lambda d, c: (0, 0))],
        out_specs=pl.BlockSpec((1, t, n_inner), lambda d, c: (d, chunk(d, c), 0)),
        out_shape=jax.ShapeDtypeStruct((2, r, n_inner), F32),
        scratch_shapes=[pltpu.VMEM((SSM_GROUPS, SSM_STATE, hpg * SSM_HEAD_DIM), F32)],
        compiler_params=_cparams(("arbitrary", "arbitrary")),
        name="ssd_scan",
    )(xbc, xbc, xbc, dt3, dt3t, dt_bias.reshape(2, 1, heads), dt_bias.reshape(2, heads, 1),
      a_log.reshape(2, 1, heads), a_log.reshape(2, heads, 1), expand)


def _gated_norm_kernel(y_ref, xs_ref, z_ref, dskip_ref, w_ref, o_ref, *, n_groups):
    z = z_ref[...]
    u = (y_ref[0] + y_ref[1] + dskip_ref[...] * xs_ref[...]) * (z * _sigmoid(z))
    gw = u.shape[1] // n_groups
    for g in range(n_groups):
        sl = slice(g * gw, (g + 1) * gw)
        ug = u[:, sl]
        ug = ug * lax.rsqrt(jnp.mean(ug * ug, axis=-1, keepdims=True) + RMS_EPS)
        o_ref[:, sl] = (ug * w_ref[:, sl]).astype(o_ref.dtype)


def _gated_norm(y, xbc, proj, dskip_w, norm_w):
    _, r, n = y.shape
    return pl.pallas_call(
        functools.partial(_gated_norm_kernel, n_groups=SSM_NORM_GROUPS),
        grid=(r // ROW_TILE,),
        in_specs=[pl.BlockSpec((2, ROW_TILE, n), lambda i: (0, i, 0)),
                  pl.BlockSpec((ROW_TILE, n), lambda i: (i, 0)),
                  pl.BlockSpec((ROW_TILE, n), lambda i: (i, 0)),
                  pl.BlockSpec((1, n), lambda i: (0, 0)),
                  pl.BlockSpec((1, n), lambda i: (0, 0))],
        out_specs=pl.BlockSpec((ROW_TILE, n), lambda i: (i, 0)),
        out_shape=jax.ShapeDtypeStruct((r, n), BF16),
        compiler_params=_cparams(("arbitrary",)),
        name="gated_norm",
    )(y, xbc, proj, dskip_w.reshape(1, n), norm_w.reshape(1, n))


def _rope_tables(n_lat, n_ctx):
    t = jnp.arange(n_lat)
    half = HEAD_DIM // 2
    inv = 1.0 / (ROPE_THETA ** (jnp.arange(0, half, 2, dtype=F32) / half))
    ang = jnp.concatenate([(t // GRID_W).astype(F32)[:, None] * inv,
                           (t % GRID_W).astype(F32)[:, None] * inv], axis=-1)
    cos = jnp.repeat(jnp.cos(ang), 2, axis=-1)
    sin = jnp.repeat(jnp.sin(ang), 2, axis=-1) * jnp.asarray(np.tile(np.array([-1.0, 1.0], np.float32), half))
    cos = jnp.concatenate([cos, jnp.ones((n_ctx, HEAD_DIM), F32)], axis=0)
    sin = jnp.concatenate([sin, jnp.zeros((n_ctx, HEAD_DIM), F32)], axis=0)
    return cos, sin


def kernel(x, c, ctx, c_ctx, w_mod, b_mod, norm1_w, norm2_w, w_mlp_in, w_mlp_out, ab_w_in, ab_conv_w, ab_conv_b,
           ab_dt_bias, ab_a_log, ab_d_skip, ab_norm_w, ab_q_norm, ab_k_norm, ab_rpb, ab_w_out, c_w_qkv, c_q_norm,
           c_k_norm, c_w_out):
    _, n_lat, d = x.shape
    n_ctx = ctx.shape[1]
    depth = w_mod.shape[0]
    assert x.shape[0] == 1 and n_ctx == ROW_TILE and n_lat % 1024 == 0
    n_inner = d
    ssm_heads = n_inner // SSM_HEAD_DIM
    conv_dim = ab_conv_w.shape[-1]
    na_heads = ab_rpb.shape[1]
    attn_heads = c_w_out.shape[1] // HEAD_DIM
    kv_heads = (c_w_qkv.shape[-1] // HEAD_DIM - attn_heads) // 2
    qk_scale = HEAD_DIM ** -0.5 * math.log2(math.e)
    i0 = n_inner
    i1 = i0 + conv_dim
    i2 = i1 + 2 * ssm_heads

    w_mlp_out_b = w_mlp_out.astype(BF16)
    ab_w_zxbc = ab_w_in[:, :, :i1]
    ab_w_dt_b = jnp.pad(ab_w_in[:, :, i1:i2], ((0, 0), (0, 0), (0, LANES - (i2 - i1)))).astype(BF16)
    ab_w_qkv = ab_w_in[:, :, i2:]

    xr = jnp.concatenate([x[0], ctx[0]], axis=0)
    vecs = jnp.zeros((SUBLANES, d), F32).at[0].set(c[0]).at[1].set(c_ctx)
    mods = _modulation(vecs, w_mod, b_mod)[:, :2].reshape(depth, 2, N_MOD, d)
    cos, sin = _rope_tables(n_lat, n_ctx)

    for layer in range(depth):
        last = layer == depth - 1
        i = layer // 2
        mod = mods[layer]
        scsh1 = jnp.stack([mod[:, 1], mod[:, 0]], axis=1)
        scsh2 = jnp.stack([mod[:, 4], mod[:, 3]], axis=1)
        rows = n_lat if last else n_lat + n_ctx
        if layer % 2 == 0:
            h = _normmod(xr, norm1_w[layer], scsh1, n_lat)
            proj = _matmul_w(h, ab_w_zxbc, i, name="ab_in_zxbc")
            dt_raw = _matmul(h, ab_w_dt_b, i, name="ab_in_dt")
            qk_w = jnp.concatenate([jnp.tile(ab_q_norm[i] * qk_scale, na_heads), jnp.tile(ab_k_norm[i], na_heads)])
            qkvb = _matmul_w(h, ab_w_qkv, i, out_dtype=BF16, epilogue="qknorm", qk_w=qk_w, name="ab_in_qkv")
            xbc = _conv_silu(proj, i0, ab_conv_w[i], ab_conv_b[i], n_lat)
            y = _ssd(xbc, dt_raw, ab_dt_bias[i], ab_a_log[i], n_lat, n_inner)
            dskip = jnp.repeat(ab_d_skip[i, 0] + ab_d_skip[i, 1], SSM_HEAD_DIM)
            gn = _gated_norm(y, xbc, proj, dskip, ab_norm_w[i])
            bias = _na_bias_tiles(ab_rpb[i], n_lat // GRID_W)
            al = _neighbourhood_attention(qkvb, bias, n_heads=na_heads, n_lat=n_lat, n_ctx=n_ctx)
            xr = _matmul_w(gn, ab_w_out, i, x2=al, rows=rows, tn=512, epilogue="resid", res=xr, gate=mod[:, 2, :],
                           n_lat=n_lat, name="mixer_out")
        else:
            p = _matmul_w(xr, c_w_qkv, i, norm=(norm1_w[layer], scsh1), n_lat=n_lat, name="c_in_qkv")
            wvec = jnp.concatenate([jnp.tile(c_q_norm[i] * qk_scale, attn_heads), jnp.tile(c_k_norm[i], kv_heads),
                                    jnp.ones((kv_heads * HEAD_DIM,), F32)])
            qk, vt = _prep(p, wvec, attn_heads + kv_heads, attn_heads + kv_heads, cos, sin, n_vt=kv_heads)
            att = _attention(qk, vt, n_heads=attn_heads, n_kv=kv_heads, n_lat=n_lat, q_rows=rows)
            xr = _matmul_w(att, c_w_out, i, rows=rows, epilogue="resid", res=xr, gate=mod[:, 2, :], n_lat=n_lat,
                           name="mixer_out")
        a = _matmul_w(xr, w_mlp_in, layer, norm=(norm2_w[layer], scsh2), out_dtype=BF16, epilogue="relu2",
                      n_lat=n_lat, name="mlp_in")
        xr = _matmul(a, w_mlp_out_b, layer, epilogue="resid", res=xr, gate=mod[:, 5, :], n_lat=n_lat,
                     name="mlp_out")
    return xr[None]
```

```python
import functools
import math

import numpy as np
import jax
import jax.numpy as jnp
from jax import lax
from jax.experimental import pallas as pl
from jax.experimental.pallas import tpu as pltpu

F32 = jnp.float32
BF16 = jnp.bfloat16

GRID_W = 64
HEAD_DIM = 128
ROPE_THETA = 10000.0
RMS_EPS = 1e-6
N_MOD = 6
SSM_HEAD_DIM = 64
SSM_GROUPS = 8
SSM_STATE = 128
SSM_NORM_GROUPS = 8
CONV_K = 5
NA_ROWS = 8
NA_COLS = 16

LANES = 128
SUBLANES = 8
VMEM_LIMIT_BYTES = 48 * 1024 * 1024
VMEM_LIMIT_WEIGHT_RESIDENT = 56 * 1024 * 1024

ROW_TILE = 256
SSD_CHUNK = 256
VT_ROWS = HEAD_DIM + 2 * SUBLANES
NA_QROWS = 4
NA_WROWS = 12
NA_HEADS_PER_STEP = 2
ATTN_Q_TILE = 512


def _pick(n, candidates):
    for c in candidates:
        if n % c == 0:
            return c
    raise ValueError(f"no tile in {candidates} divides {n}")


def _cparams(sem):
    return pltpu.CompilerParams(dimension_semantics=sem, vmem_limit_bytes=VMEM_LIMIT_BYTES)


def _sigmoid(v):
    return 1.0 / (1.0 + jnp.exp(-v))


def _softplus(v):
    return jnp.maximum(v, 0.0) + jnp.log(1.0 + jnp.exp(-jnp.abs(v)))


def _dot(a, b):
    return jnp.dot(a, b, preferred_element_type=F32)


def _dot_nt(a, b):
    return lax.dot_general(a, b, (((1,), (1,)), ((), ())), preferred_element_type=F32)


def _split3(x):
    hi = x.astype(BF16)
    r1 = x - hi.astype(F32)
    mid = r1.astype(BF16)
    lo = (r1 - mid.astype(F32)).astype(BF16)
    return hi, mid, lo


def _mod_kernel(v_ref, w_ref, b_ref, o_ref):
    v = v_ref[...]
    s = (v * _sigmoid(v)).astype(BF16)
    o_ref[0] = _dot(s, w_ref[0].astype(BF16)) + b_ref[0]


def _modulation(vecs, w_mod, b_mod):
    depth, d, n = w_mod.shape
    tn = _pick(n, (1024, 512, 256, 128))
    return pl.pallas_call(
        _mod_kernel,
        grid=(depth, n // tn),
        in_specs=[pl.BlockSpec((SUBLANES, d), lambda l, j: (0, 0)),
                  pl.BlockSpec((1, d, tn), lambda l, j: (l, 0, j)),
                  pl.BlockSpec((1, 1, tn), lambda l, j: (l, 0, j))],
        out_specs=pl.BlockSpec((1, SUBLANES, tn), lambda l, j: (l, 0, j)),
        out_shape=jax.ShapeDtypeStruct((depth, SUBLANES, n), F32),
        compiler_params=_cparams(("arbitrary", "arbitrary")),
        name="modulation",
    )(vecs, w_mod, b_mod.reshape(depth, 1, n))


def _normmod_kernel(x_ref, w_ref, m_ref, o_ref):
    x = x_ref[...]
    y = x * lax.rsqrt(jnp.mean(x * x, axis=-1, keepdims=True) + RMS_EPS)
    m = m_ref[0]
    o_ref[...] = ((y * w_ref[...]) * (1.0 + m[0:1, :]) + m[1:2, :]).astype(o_ref.dtype)


def _normmod(x, w, scsh, n_lat):
    r, d = x.shape
    nlb = n_lat // ROW_TILE
    return pl.pallas_call(
        _normmod_kernel,
        grid=(r // ROW_TILE,),
        in_specs=[pl.BlockSpec((ROW_TILE, d), lambda i: (i, 0)),
                  pl.BlockSpec((1, d), lambda i: (0, 0)),
                  pl.BlockSpec((1, 2, d), lambda i: (i // nlb, 0, 0))],
        out_specs=pl.BlockSpec((ROW_TILE, d), lambda i: (i, 0)),
        out_shape=jax.ShapeDtypeStruct((r, d), BF16),
        compiler_params=_cparams(("arbitrary",)),
        name="normmod",
    )(x, w.reshape(1, d), scsh)


def _mm_kernel(*refs, nk, nk1, two_lhs, epilogue, n_lat, tm):
    refs = list(refs)
    x_ref = refs.pop(0)
    x2_ref = refs.pop(0) if two_lhs else None
    w_ref = refs.pop(0)
    if epilogue == "resid":
        res_ref = refs.pop(0)
        gate_ref = refs.pop(0)
    o_ref = refs.pop(0)

    def finish(acc):
        if epilogue == "relu2":
            a = jnp.maximum(acc, 0.0)
            o_ref[...] = (a * a).astype(o_ref.dtype)
        elif epilogue == "resid":
            row = pl.program_id(0) * tm + lax.broadcasted_iota(jnp.int32, (tm, 1), 0)
            g = jnp.where(row < n_lat, gate_ref[0:1, :], gate_ref[1:2, :])
            o_ref[...] = res_ref[...] + g * acc
        else:
            o_ref[...] = acc.astype(o_ref.dtype)

    if nk == 1:
        finish(_dot(x_ref[...], w_ref[0]))
        return
    acc_ref, = refs
    k = pl.program_id(2)

    @pl.when(k == 0)
    def _():
        acc_ref[...] = jnp.zeros_like(acc_ref)

    if two_lhs:
        @pl.when(k < nk1)
        def _():
            acc_ref[...] += _dot(x_ref[...], w_ref[0])

        @pl.when(k >= nk1)
        def _():
            acc_ref[...] += _dot(x2_ref[...], w_ref[0])
    else:
        acc_ref[...] += _dot(x_ref[...], w_ref[0])

    @pl.when(k == nk - 1)
    def _():
        finish(acc_ref[...])


def _matmul(x, w, layer, *, x2=None, rows=None, n_cols=None, out_dtype=F32, epilogue="none", res=None, gate=None,
            n_lat=0, name="matmul"):
    r = x.shape[0] if rows is None else rows
    _, kdim, n_w = w.shape
    n = n_w if n_cols is None else n_cols
    k1 = x.shape[1]
    tm = _pick(r, (1024, 768, 512, 256))
    tn = _pick(n, (1024, 512, 256, 128))
    if x2 is None:
        tk = kdim if kdim <= 2048 else _pick(kdim, (2048, 1024, 512))
    else:
        tk = math.gcd(k1, x2.shape[1])
    nk, nk1 = kdim // tk, k1 // tk
    in_specs = [pl.BlockSpec((tm, tk), lambda i, j, k: (i, jnp.minimum(k, nk1 - 1)))]
    args = [x]
    if x2 is not None:
        in_specs.append(pl.BlockSpec((tm, tk), lambda i, j, k: (i, jnp.maximum(k - nk1, 0))))
        args.append(x2)
    in_specs.append(pl.BlockSpec((1, tk, tn), lambda i, j, k: (layer, k, j)))
    args.append(w)
    if epilogue == "resid":
        in_specs += [pl.BlockSpec((tm, tn), lambda i, j, k: (i, j)),
                     pl.BlockSpec((2, tn), lambda i, j, k: (0, j))]
        args += [res, gate]
    kernel = functools.partial(_mm_kernel, nk=nk, nk1=nk1, two_lhs=x2 is not None, epilogue=epilogue, n_lat=n_lat,
                               tm=tm)
    return pl.pallas_call(
        kernel,
        grid=(r // tm, n // tn, nk),
        in_specs=in_specs,
        out_specs=pl.BlockSpec((tm, tn), lambda i, j, k: (i, j)),
        out_shape=jax.ShapeDtypeStruct((r, n), out_dtype),
        scratch_shapes=[] if nk == 1 else [pltpu.VMEM((tm, tn), F32)],
        compiler_params=_cparams(("arbitrary", "arbitrary", "arbitrary")),
        name=name,
    )(*args)


def _mmw_kernel(*refs, prologue, two_lhs, w_rows, epilogue, n_lat, tm, k1, qk_tiles):
    refs = list(refs)
    x_ref = refs.pop(0)
    if prologue:
        nw_ref = refs.pop(0)
        mod_ref = refs.pop(0)
    x2_ref = refs.pop(0) if two_lhs else None
    w_ref = refs.pop(0)
    if epilogue == "resid":
        res_ref = refs.pop(0)
        gate_ref = refs.pop(0)
    if epilogue == "qknorm":
        qkw_ref = refs.pop(0)
    o_ref, wb_ref = refs[:2]
    j = pl.program_id(0)
    i = pl.program_id(1)

    @pl.when(i == 0)
    def _():
        wb_ref[...] = w_ref[0].astype(BF16)

    if prologue:
        h_ref = refs[2]
        for s in range(tm // ROW_TILE):
            rs = slice(s * ROW_TILE, (s + 1) * ROW_TILE)
            ctx_rows = i * tm + s * ROW_TILE >= n_lat
            scale = nw_ref[...] * (1.0 + jnp.where(ctx_rows, mod_ref[1, 0:1, :], mod_ref[0, 0:1, :]))
            shift = jnp.where(ctx_rows, mod_ref[1, 1:2, :], mod_ref[0, 1:2, :])
            x = x_ref[rs, :]
            y = x * lax.rsqrt(jnp.mean(x * x, axis=-1, keepdims=True) + RMS_EPS)
            h_ref[rs, :] = (y * scale + shift).astype(BF16)
        h = h_ref[...]
    else:
        h = x_ref[...]
    if two_lhs:
        acc = _dot(h, wb_ref[0:k1, :]) + _dot(x2_ref[...], wb_ref[k1:, :])
    elif w_rows:
        acc = _dot_nt(h, wb_ref[...])
    else:
        acc = _dot(h, wb_ref[...])
    if epilogue == "relu2":
        a = jnp.maximum(acc, 0.0)
        o_ref[...] = (a * a).astype(o_ref.dtype)
    elif epilogue == "resid":
        is_lat = (i * tm + lax.broadcasted_iota(jnp.int32, (tm, 1), 0)) < n_lat
        o_ref[...] = res_ref[...] + jnp.where(is_lat, gate_ref[0:1, :], gate_ref[1:2, :]) * acc
    elif epilogue == "qknorm":
        @pl.when(j < qk_tiles)
        def _():
            for b in range(acc.shape[1] // HEAD_DIM):
                sl = slice(b * HEAD_DIM, (b + 1) * HEAD_DIM)
                u = acc[:, sl]
                u = u * lax.rsqrt(jnp.mean(u * u, axis=-1, keepdims=True) + RMS_EPS) * qkw_ref[:, sl]
                o_ref[:, sl] = u.astype(o_ref.dtype)

        @pl.when(j >= qk_tiles)
        def _():
            o_ref[...] = acc.astype(o_ref.dtype)
    else:
        o_ref[...] = acc.astype(o_ref.dtype)


def _matmul_w(x, w, layer, *, norm=None, x2=None, rows=None, w_rows=False, col0=0, n_cols=None, tn=1024, out_dtype=F32,
              epilogue="none", res=None, gate=None, qk_w=None, n_lat=0, name="matmul_w"):
    r = x.shape[0] if rows is None else rows
    kdim, n_w = (w.shape[2], w.shape[1]) if w_rows else (w.shape[1], w.shape[2])
    n = n_w - col0 if n_cols is None else n_cols
    tn = min(tn, n)
    k1 = x.shape[1]
    tm = _pick(r, (1024, 768, 512, 256))
    in_specs = [pl.BlockSpec((tm, k1), lambda j, i: (i, 0))]
    args = [x]
    scratch = [pltpu.VMEM((tn, kdim) if w_rows else (kdim, tn), BF16)]
    if norm is not None:
        norm_w, scsh = norm
        in_specs += [pl.BlockSpec((1, k1), lambda j, i: (0, 0)),
                     pl.BlockSpec((2, 2, k1), lambda j, i: (0, 0, 0))]
        args += [norm_w.reshape(1, k1), scsh]
        scratch.append(pltpu.VMEM((tm, k1), BF16))
    if x2 is not None:
        in_specs.append(pl.BlockSpec((tm, x2.shape[1]), lambda j, i: (i, 0)))
        args.append(x2)
    if w_rows:
        in_specs.append(pl.BlockSpec((pl.Element(1), pl.Element(tn), pl.Element(kdim)),
                                     lambda j, i: (layer, pl.multiple_of(col0 + j * tn, SUBLANES), 0)))
    else:
        assert col0 == 0
        in_specs.append(pl.BlockSpec((1, kdim, tn), lambda j, i: (layer, 0, j)))
    args.append(w)
    if epilogue == "resid":
        in_specs += [pl.BlockSpec((tm, tn), lambda j, i: (i, j)),
                     pl.BlockSpec((2, tn), lambda j, i: (0, j))]
        args += [res, gate]
    qk_tiles = 0
    if epilogue == "qknorm":
        qk_tiles = qk_w.shape[0] // tn
        assert qk_w.shape[0] % tn == 0
        in_specs.append(pl.BlockSpec((1, tn), lambda j, i: (0, jnp.minimum(j, qk_tiles - 1))))
        args.append(qk_w.reshape(1, -1))
    kernel = functools.partial(_mmw_kernel, prologue=norm is not None, two_lhs=x2 is not None, w_rows=w_rows,
                               epilogue=epilogue, n_lat=n_lat, tm=tm, k1=k1, qk_tiles=qk_tiles)
    return pl.pallas_call(
        kernel,
        grid=(n // tn, r // tm),
        in_specs=in_specs,
        out_specs=pl.BlockSpec((tm, tn), lambda j, i: (i, j)),
        out_shape=jax.ShapeDtypeStruct((r, n), out_dtype),
        scratch_shapes=scratch,
        compiler_params=pltpu.CompilerParams(dimension_semantics=("arbitrary", "arbitrary"),
                                             vmem_limit_bytes=VMEM_LIMIT_WEIGHT_RESIDENT),
        name=name,
    )(*args)


def _prep_kernel(*refs, n_norm, n_rope, n_blocks, n_vt):
    refs = list(refs)
    p_ref, w_ref = refs[:2]
    if n_rope:
        cos = refs[2][...]
        sin = refs[3][...]
        even = (lax.broadcasted_iota(jnp.int32, cos.shape, 1) % 2) == 0
    o_ref = refs[-2] if n_vt else refs[-1]
    for b in range(n_blocks):
        sl = slice(b * HEAD_DIM, (b + 1) * HEAD_DIM)
        u = p_ref[:, sl]
        if b < n_norm:
            u = u * lax.rsqrt(jnp.mean(u * u, axis=-1, keepdims=True) + RMS_EPS) * w_ref[:, sl]
        if b < n_rope:
            partner = jnp.where(even, pltpu.roll(u, HEAD_DIM - 1, 1), pltpu.roll(u, 1, 1))
            u = u * cos + partner * sin
        if b < n_blocks - n_vt:
            o_ref[:, sl] = u.astype(o_ref.dtype)
        else:
            vt_ref = refs[-1]
            hv = b - (n_blocks - n_vt)
            vt_ref[hv, 0, 0:HEAD_DIM, :] = u.T.astype(BF16)
            pad_row = lax.broadcasted_iota(jnp.int32, (VT_ROWS - HEAD_DIM, u.shape[0]), 0)
            vt_ref[hv, 0, HEAD_DIM:, :] = jnp.where(pad_row == 0, 1.0, 0.0).astype(BF16)


def _attn_key_tile(n_all):
    for tk in (768, ROW_TILE):
        if n_all % tk == 0 and (n_all // tk) % 2 == 1:
            return tk
    raise ValueError(f"no key tile for {n_all} rows")


def _prep(p, wvec, n_norm, n_rope=0, cos=None, sin=None, n_vt=0):
    r, n = p.shape
    n_blocks = n // HEAD_DIM
    n_out = (n_blocks - n_vt) * HEAD_DIM
    tk = _attn_key_tile(r) if n_vt else ROW_TILE
    per = tk // ROW_TILE
    in_specs = [pl.BlockSpec((ROW_TILE, n), lambda i: (i, 0)),
                pl.BlockSpec((1, n), lambda i: (0, 0))]
    args = [p, wvec.reshape(1, n)]
    if n_rope:
        in_specs += [pl.BlockSpec((ROW_TILE, HEAD_DIM), lambda i: (i, 0))] * 2
        args += [cos, sin]
    out_specs = [pl.BlockSpec((ROW_TILE, n_out), lambda i: (i, 0))]
    out_shape = [jax.ShapeDtypeStruct((r, n_out), BF16)]
    if n_vt:
        out_specs.append(pl.BlockSpec((n_vt, 1, VT_ROWS, ROW_TILE), lambda i: (0, i // per, 0, i % per)))
        out_shape.append(jax.ShapeDtypeStruct((n_vt, r // tk, VT_ROWS, tk), BF16))
    out = pl.pallas_call(
        functools.partial(_prep_kernel, n_norm=n_norm, n_rope=n_rope, n_blocks=n_blocks, n_vt=n_vt),
        grid=(r // ROW_TILE,),
        in_specs=in_specs,
        out_specs=out_specs,
        out_shape=out_shape,
        compiler_params=_cparams(("arbitrary",)),
        name="qkv_prep",
    )(*args)
    return out if n_vt else out[0]


def _attn_kernel(q_ref, k_ref, vt_ref, o_ref, s0_ref, s1_ref, m_ref, acc_ref, *, group, tq, tk, n_lat, n_all):
    n_tiles = n_all // tk
    n_ctx = n_all - n_lat
    m_ref[...] = jnp.full(m_ref.shape, -jnp.inf, F32)
    acc_ref[...] = jnp.zeros(acc_ref.shape, F32)

    def scores(kc, s_ref, g):
        st = _dot_nt(kc, q_ref[:, g * HEAD_DIM:(g + 1) * HEAD_DIM])
        s_ref[g, 0:kc.shape[0], :] = st
        s_ref[g, tk:tk + 1, :] = jnp.max(st, axis=0, keepdims=True)

    def consume(vtc, s_ref, g):
        st = s_ref[g, 0:vtc.shape[1], :]
        m_prev = m_ref[g]
        m_new = jnp.maximum(m_prev, s_ref[g, tk:tk + 1, :])
        p = jnp.exp2(st - m_new).astype(BF16)
        acc_ref[g] = jnp.exp2(m_prev - m_new) * acc_ref[g] + _dot(vtc, p)
        m_ref[g] = m_new

    def key_tile(c):
        return k_ref[pl.ds(pl.multiple_of(c * tk, tk), tk), :]

    def step(k_next, vt_cur, s_next, s_cur):
        for g in range(group):
            if k_next is not None:
                scores(k_next, s_next, g)
            if vt_cur is not None:
                consume(vt_cur, s_cur, g)

    def pair(i, carry):
        c = 2 * i
        step(key_tile(c + 1), vt_ref[0, c], s1_ref, s0_ref)
        step(key_tile(c + 2), vt_ref[0, c + 1], s0_ref, s1_ref)
        return carry

    @pl.when(pl.program_id(1) < n_lat // tq)
    def _():
        step(key_tile(0), None, s0_ref, None)
        lax.fori_loop(0, (n_tiles - 1) // 2, pair, 0)
        step(None, vt_ref[0, n_tiles - 1], None, s0_ref)

    @pl.when(pl.program_id(1) >= n_lat // tq)
    def _():
        step(k_ref[n_lat:n_all, :], None, s0_ref, None)
        step(None, vt_ref[0, n_tiles - 1, :, tk - n_ctx:tk], None, s0_ref)

    for g in range(group):
        out = acc_ref[g, 0:HEAD_DIM, :] / acc_ref[g, HEAD_DIM:HEAD_DIM + 1, :]
        o_ref[:, g * HEAD_DIM:(g + 1) * HEAD_DIM] = out.T.astype(o_ref.dtype)


def _attention(qk, vt, *, n_heads, n_kv, n_lat, q_rows):
    n_all = qk.shape[0]
    group = n_heads // n_kv
    tq = ATTN_Q_TILE
    tk = vt.shape[-1]
    gw = group * HEAD_DIM
    assert n_all - n_lat <= min(tk, tq) and n_lat % tq == 0
    kernel = functools.partial(_attn_kernel, group=group, tq=tq, tk=tk, n_lat=n_lat, n_all=n_all)
    return pl.pallas_call(
        kernel,
        grid=(n_kv, pl.cdiv(q_rows, tq)),
        in_specs=[pl.BlockSpec((tq, gw), lambda h, i: (i, h)),
                  pl.BlockSpec((n_all, HEAD_DIM), lambda h, i: (0, n_heads + h)),
                  pl.BlockSpec((1, n_all // tk, VT_ROWS, tk), lambda h, i: (h, 0, 0, 0))],
        out_specs=pl.BlockSpec((tq, gw), lambda h, i: (i, h)),
        out_shape=jax.ShapeDtypeStruct((q_rows, n_heads * HEAD_DIM), BF16),
        scratch_shapes=[pltpu.VMEM((group, tk + SUBLANES, tq), F32),
                        pltpu.VMEM((group, tk + SUBLANES, tq), F32),
                        pltpu.VMEM((group, 1, tq), F32),
                        pltpu.VMEM((group, VT_ROWS, tq), F32)],
        compiler_params=_cparams(("arbitrary", "arbitrary")),
        name="gqa_attention",
    )(qk, qk, vt)


def _na_kernel(q_ref, k_ref, v_ref, bias_ref, o_ref, *, n_lat, rows, n_ctx):
    nb = rows // NA_QROWS
    b = pl.program_id(1)
    heads = [slice(j * HEAD_DIM, (j + 1) * HEAD_DIM) for j in range(NA_HEADS_PER_STEP)]
    s_c = [_dot_nt(q_ref[:, hs], k_ref[n_lat:n_lat + n_ctx, hs]) for hs in heads]

    @pl.when(b < nb)
    def _():
        w0 = jnp.clip(NA_QROWS * b - NA_ROWS // 2, 0, rows - NA_WROWS)
        win = pl.ds(pl.multiple_of(w0 * GRID_W, GRID_W), NA_WROWS * GRID_W)
        s_w = [_dot_nt(q_ref[:, hs], k_ref[win, hs]) + bias_ref[0, j] for j, hs in enumerate(heads)]
        m = [jnp.maximum(jnp.max(sw, axis=1, keepdims=True), jnp.max(sc, axis=1, keepdims=True))
             for sw, sc in zip(s_w, s_c)]
        p_w = [jnp.exp2(sw - mj) for sw, mj in zip(s_w, m)]
        p_c = [jnp.exp2(sc - mj) for sc, mj in zip(s_c, m)]
        for j, hs in enumerate(heads):
            l = jnp.sum(p_w[j], axis=1, keepdims=True) + jnp.sum(p_c[j], axis=1, keepdims=True)
            o = _dot(p_w[j].astype(BF16), v_ref[win, hs]) + _dot(p_c[j].astype(BF16), v_ref[n_lat:n_lat + n_ctx, hs])
            o_ref[:, hs] = (o / l).astype(o_ref.dtype)

    @pl.when(b >= nb)
    def _():
        for j, hs in enumerate(heads):
            p_c = jnp.exp2(s_c[j] - jnp.max(s_c[j], axis=1, keepdims=True))
            l = jnp.sum(p_c, axis=1, keepdims=True)
            o_ref[:, hs] = (_dot(p_c.astype(BF16), v_ref[n_lat:n_lat + n_ctx, hs]) / l).astype(o_ref.dtype)


def _na_bias_tiles(rpb, rows):
    h = rpb.shape[0]
    col = np.arange(GRID_W)
    c0 = np.clip(col - NA_COLS // 2, 0, GRID_W - NA_COLS)
    col_ok = (col[None, :] >= c0[:, None]) & (col[None, :] < c0[:, None] + NA_COLS)
    pad = GRID_W - NA_COLS
    rp = jnp.pad(rpb * math.log2(math.e), ((0, 0), (0, 0), (pad, pad)))
    toep = jnp.stack([rp[:, :, NA_COLS - 1 - q + pad:NA_COLS - 1 - q + pad + GRID_W] for q in range(GRID_W)], axis=2)
    toep = jnp.where(col_ok[None, None], toep, -jnp.inf)
    neg = jnp.full((h, GRID_W, GRID_W), -jnp.inf, F32)
    half = NA_ROWS // 2
    kinds = (lambda j: (j, 0), lambda j: (half + j, j), lambda j: (NA_ROWS + j, half))
    tiles = []
    for kind in kinds:
        qrows = []
        for j in range(NA_QROWS):
            q_row, win = kind(j)
            blocks = []
            for ik in range(NA_WROWS):
                dr = ik - q_row + (NA_ROWS - 1)
                ok = win <= ik < win + NA_ROWS
                blocks.append(toep[:, dr] if ok else neg)
            qrows.append(jnp.concatenate(blocks, axis=-1))
        tiles.append(jnp.concatenate(qrows, axis=-2))
    return jnp.stack(tiles, axis=0)


def _neighbourhood_attention(qkv, bias, *, n_heads, n_lat, n_ctx):
    r = qkv.shape[0]
    rows = n_lat // GRID_W
    nb = rows // NA_QROWS
    tq = NA_QROWS * GRID_W
    assert tq == n_ctx and rows >= NA_WROWS + NA_QROWS

    def bias_map(h, b):
        return (jnp.where(b == 0, 0, jnp.where(b >= nb - 1, 2, 1)), h, 0, 0)

    kernel = functools.partial(_na_kernel, n_lat=n_lat, rows=rows, n_ctx=n_ctx)
    hps = NA_HEADS_PER_STEP
    hw = hps * HEAD_DIM
    n_steps = n_heads // hps
    return pl.pallas_call(
        kernel,
        grid=(n_steps, r // tq),
        in_specs=[pl.BlockSpec((tq, hw), lambda h, b: (b, h)),
                  pl.BlockSpec((r, hw), lambda h, b: (0, n_steps + h)),
                  pl.BlockSpec((r, hw), lambda h, b: (0, 2 * n_steps + h)),
                  pl.BlockSpec((1, hps, tq, NA_WROWS * GRID_W), bias_map)],
        out_specs=pl.BlockSpec((tq, hw), lambda h, b: (b, h)),
        out_shape=jax.ShapeDtypeStruct((r, n_heads * HEAD_DIM), BF16),
        compiler_params=_cparams(("arbitrary", "arbitrary")),
        name="neighbourhood_attention",
    )(qkv, qkv, qkv, bias)


def _conv_kernel(u_ref, prev_ref, next_ref, w_ref, b_ref, o_ref, ext_ref, *, n_lat):
    tr = u_ref.shape[0]
    i = pl.program_id(0)
    nlb = n_lat // tr
    has_prev = jnp.logical_and(i != 0, i != nlb)
    has_next = i < nlb - 1
    ext_ref[0:SUBLANES, :] = jnp.where(has_prev, prev_ref[...], 0.0)
    ext_ref[SUBLANES:SUBLANES + tr, :] = u_ref[...]
    ext_ref[SUBLANES + tr:, :] = jnp.where(has_next, next_ref[...], 0.0)
    ext = ext_ref[...]
    n_ext = ext.shape[0]
    acc = jnp.broadcast_to(b_ref[...], o_ref.shape)
    for j in range(CONV_K):
        d = j - CONV_K // 2
        shifted = ext if d == 0 else pltpu.roll(ext, (n_ext - d) % n_ext, 0)
        acc = acc + w_ref[j:j + 1, :] * shifted[SUBLANES:SUBLANES + tr, :]
    o_ref[...] = acc * _sigmoid(acc)


def _conv_silu(proj, col0, conv_w, conv_b, n_lat):
    r = proj.shape[0]
    ch = conv_w.shape[1]
    tr = ROW_TILE
    tc = _pick(ch, (1024, 512, 256, 128))
    cb0 = col0 // tc
    assert col0 % tc == 0
    rb = tr // SUBLANES
    last = r // SUBLANES - 1
    return pl.pallas_call(
        functools.partial(_conv_kernel, n_lat=n_lat),
        grid=(r // tr, ch // tc),
        in_specs=[pl.BlockSpec((tr, tc), lambda i, j: (i, cb0 + j)),
                  pl.BlockSpec((SUBLANES, tc), lambda i, j: (jnp.maximum(i * rb - 1, 0), cb0 + j)),
                  pl.BlockSpec((SUBLANES, tc), lambda i, j: (jnp.minimum((i + 1) * rb, last), cb0 + j)),
                  pl.BlockSpec((CONV_K, tc), lambda i, j: (0, j)),
                  pl.BlockSpec((1, tc), lambda i, j: (0, j))],
        out_specs=pl.BlockSpec((tr, tc), lambda i, j: (i, j)),
        out_shape=jax.ShapeDtypeStruct((r, ch), F32),
        scratch_shapes=[pltpu.VMEM((tr + 2 * SUBLANES, tc), F32)],
        compiler_params=_cparams(("arbitrary", "arbitrary")),
        name="conv_silu",
    )(proj, proj, proj, conv_w, conv_b.reshape(1, ch))


def _ssd_kernel(xs_ref, b_ref, c_ref, dt_ref, dtt_ref, bias_ref, biast_ref, alog_ref, alogt_ref, e_ref,
                y_ref, st_ref, *, n_groups, heads_per_group):
    t = xs_ref.shape[0]
    d = pl.program_id(0)
    c = pl.program_id(1)
    gw = heads_per_group * SSM_HEAD_DIM

    @pl.when(c == 0)
    def _():
        st_ref[...] = jnp.zeros(st_ref.shape, F32)

    dt = _softplus(dt_ref[0] + bias_ref[0])
    dtt = _softplus(dtt_ref[0] + biast_ref[0])
    da = dt * (-jnp.exp(alog_ref[0]))
    dat = dtt * (-jnp.exp(alogt_ref[0]))
    row = lax.broadcasted_iota(jnp.int32, (t, t), 0)
    col = lax.broadcasted_iota(jnp.int32, (t, t), 1)
    lag = jnp.where(d == 0, row - col, col - row)
    valid = lag >= 0
    ones_v = jnp.where(valid, 1.0, 0.0).astype(BF16)
    ones_vt = jnp.where(lag <= 0, 1.0, 0.0).astype(BF16)
    cum = sum(_dot(ones_v, piece) for piece in _split3(da))
    cumt = sum(_dot(piece, ones_vt) for piece in _split3(dat))
    tot = jnp.sum(da, axis=0, keepdims=True)
    cumt_in = cumt - jnp.log(dtt)
    stack = jnp.concatenate([dt * jnp.exp(tot - cum), jnp.exp(cum),
                             jnp.broadcast_to(jnp.exp(tot), (SUBLANES, tot.shape[1]))], axis=0)
    wide = _dot(jnp.concatenate(_split3(stack), axis=1), e_ref[...])
    win_w = wide[0:t]
    ecum_w = wide[t:2 * t]
    cdec_w = wide[2 * t:2 * t + 1]
    xs = xs_ref[...]
    xin = (xs * win_w).astype(BF16)
    lane_head = lax.broadcasted_iota(jnp.int32, (t, gw), 1) // SSM_HEAD_DIM

    for g in range(n_groups):
        ns = slice(g * SSM_STATE, (g + 1) * SSM_STATE)
        hs = slice(g * gw, (g + 1) * gw)
        bg = b_ref[:, ns]
        cg = c_ref[:, ns].astype(BF16)
        cb = _dot_nt(cg, bg.astype(BF16))
        acc = _dot(cg, st_ref[g].astype(BF16)) * ecum_w[:, hs]
        xs_g = xs[:, hs].astype(BF16)
        for k in range(heads_per_group):
            h = g * heads_per_group + k
            seg = cum[:, h:h + 1] - cumt_in[h:h + 1, :]
            decay = jnp.exp(jnp.where(valid, seg, -jnp.inf))
            mh = (cb * decay).astype(BF16)
            xk = jnp.where(lane_head == k, xs_g, jnp.zeros_like(xs_g))
            acc = acc + _dot(mh, xk)
        y_ref[0, :, hs] = acc
        st_ref[g] = st_ref[g] * cdec_w[:, hs] + _dot(bg.T.astype(BF16), xin[:, hs])


def _ssd(xbc, dt_raw, dt_bias, a_log, n_lat, n_inner):
    r = xbc.shape[0]
    t = SSD_CHUNK
    heads = n_inner // SSM_HEAD_DIM
    gn = SSM_GROUPS * SSM_STATE
    hpg = heads // SSM_GROUPS
    n_lc = n_lat // t
    n_cc = (r - n_lat) // t
    assert (r - n_lat) % t == 0 and n_inner % gn == 0
    dt3 = dt_raw[:, :2 * heads].reshape(r, 2, heads).transpose(1, 0, 2)
    dt3t = dt3.transpose(0, 2, 1)
    expand = jnp.asarray(np.tile(np.repeat(np.eye(heads, dtype=np.float32), SSM_HEAD_DIM, axis=1), (3, 1)), BF16)

    def chunk(d, c):
        fwd = jnp.where(c < n_cc, n_lc + c, c - n_cc)
        bwd = n_lc + n_cc - 1 - c
        return jnp.where(d == 0, fwd, bwd)

    xb = n_inner // gn
    kernel = functools.partial(_ssd_kernel, n_groups=SSM_GROUPS, heads_per_group=hpg)
    return pl.pallas_call(
        kernel,
        grid=(2, n_lc + n_cc),
        in_specs=[pl.BlockSpec((t, n_inner), lambda d, c: (chunk(d, c), 0)),
                  pl.BlockSpec((t, gn), lambda d, c: (chunk(d, c), xb)),
                  pl.BlockSpec((t, gn), lambda d, c: (chunk(d, c), xb + 1)),
                  pl.BlockSpec((1, t, heads), lambda d, c: (d, chunk(d, c), 0)),
                  pl.BlockSpec((1, heads, t), lambda d, c: (d, 0, chunk(d, c))),
                  pl.BlockSpec((1, 1, heads), lambda d, c: (d, 0, 0)),
                  pl.BlockSpec((1, heads, 1), lambda d, c: (d, 0, 0)),
                  pl.BlockSpec((1, 1, heads), lambda d, c: (d, 0, 0)),
                  pl.BlockSpec((1, heads, 1), lambda d, c: (d, 0, 0)),
                  pl.BlockSpec((3 * heads, n_inner), lambda d, c: (0, 0))],
        out_specs=pl.BlockSpec((1, t, n_inner), lambda d, c: (d, chunk(d, c), 0)),
        out_shape=jax.ShapeDtypeStruct((2, r, n_inner), F32),
        scratch_shapes=[pltpu.VMEM((SSM_GROUPS, SSM_STATE, hpg * SSM_HEAD_DIM), F32)],
        compiler_params=_cparams(("arbitrary", "arbitrary")),
        name="ssd_scan",
    )(xbc, xbc, xbc, dt3, dt3t, dt_bias.reshape(2, 1, heads), dt_bias.reshape(2, heads, 1),
      a_log.reshape(2, 1, heads), a_log.reshape(2, heads, 1), expand)


def _gated_norm_kernel(y_ref, xs_ref, z_ref, dskip_ref, w_ref, o_ref, *, n_groups):
    z = z_ref[...]
    u = (y_ref[0] + y_ref[1] + dskip_ref[...] * xs_ref[...]) * (z * _sigmoid(z))
    gw = u.shape[1] // n_groups
    for g in range(n_groups):
        sl = slice(g * gw, (g + 1) * gw)
        ug = u[:, sl]
        ug = ug * lax.rsqrt(jnp.mean(ug * ug, axis=-1, keepdims=True) + RMS_EPS)
        o_ref[:, sl] = (ug * w_ref[:, sl]).astype(o_ref.dtype)


def _gated_norm(y, xbc, proj, dskip_w, norm_w):
    _, r, n = y.shape
    return pl.pallas_call(
        functools.partial(_gated_norm_kernel, n_groups=SSM_NORM_GROUPS),
        grid=(r // ROW_TILE,),
        in_specs=[pl.BlockSpec((2, ROW_TILE, n), lambda i: (0, i, 0)),
                  pl.BlockSpec((ROW_TILE, n), lambda i: (i, 0)),
                  pl.BlockSpec((ROW_TILE, n), lambda i: (i, 0)),
                  pl.BlockSpec((1, n), lambda i: (0, 0)),
                  pl.BlockSpec((1, n), lambda i: (0, 0))],
        out_specs=pl.BlockSpec((ROW_TILE, n), lambda i: (i, 0)),
        out_shape=jax.ShapeDtypeStruct((r, n), BF16),
        compiler_params=_cparams(("arbitrary",)),
        name="gated_norm",
    )(y, xbc, proj, dskip_w.reshape(1, n), norm_w.reshape(1, n))


def _rope_tables(n_lat, n_ctx):
    t = jnp.arange(n_lat)
    half = HEAD_DIM // 2
    inv = 1.0 / (ROPE_THETA ** (jnp.arange(0, half, 2, dtype=F32) / half))
    ang = jnp.concatenate([(t // GRID_W).astype(F32)[:, None] * inv,
                           (t % GRID_W).astype(F32)[:, None] * inv], axis=-1)
    cos = jnp.repeat(jnp.cos(ang), 2, axis=-1)
    sin = jnp.repeat(jnp.sin(ang), 2, axis=-1) * jnp.asarray(np.tile(np.array([-1.0, 1.0], np.float32), half))
    cos = jnp.concatenate([cos, jnp.ones((n_ctx, HEAD_DIM), F32)], axis=0)
    sin = jnp.concatenate([sin, jnp.zeros((n_ctx, HEAD_DIM), F32)], axis=0)
    return cos, sin


def kernel(x, c, ctx, c_ctx, w_mod, b_mod, norm1_w, norm2_w, w_mlp_in, w_mlp_out, ab_w_in, ab_conv_w, ab_conv_b,
           ab_dt_bias, ab_a_log, ab_d_skip, ab_norm_w, ab_q_norm, ab_k_norm, ab_rpb, ab_w_out, c_w_qkv, c_q_norm,
           c_k_norm, c_w_out):
    _, n_lat, d = x.shape
    n_ctx = ctx.shape[1]
    depth = w_mod.shape[0]
    assert x.shape[0] == 1 and n_ctx == ROW_TILE and n_lat % 1024 == 0
    n_inner = d
    ssm_heads = n_inner // SSM_HEAD_DIM
    conv_dim = ab_conv_w.shape[-1]
    na_heads = ab_rpb.shape[1]
    attn_heads = c_w_out.shape[1] // HEAD_DIM
    kv_heads = (c_w_qkv.shape[-1] // HEAD_DIM - attn_heads) // 2
    qk_scale = HEAD_DIM ** -0.5 * math.log2(math.e)
    i0 = n_inner
    i1 = i0 + conv_dim
    i2 = i1 + 2 * ssm_heads

    w_mlp_out_b = w_mlp_out.astype(BF16)
    ab_w_in_t = jnp.swapaxes(ab_w_in, 1, 2)

    xr = jnp.concatenate([x[0], ctx[0]], axis=0)
    vecs = jnp.zeros((SUBLANES, d), F32).at[0].set(c[0]).at[1].set(c_ctx)
    mods = _modulation(vecs, w_mod, b_mod)[:, :2].reshape(depth, 2, N_MOD, d)
    cos, sin = _rope_tables(n_lat, n_ctx)

    for layer in range(depth):
        last = layer == depth - 1
        i = layer // 2
        mod = mods[layer]
        scsh1 = jnp.stack([mod[:, 1], mod[:, 0]], axis=1)
        scsh2 = jnp.stack([mod[:, 4], mod[:, 3]], axis=1)
        rows = n_lat if last else n_lat + n_ctx
        if layer % 2 == 0:
            h = _normmod(xr, norm1_w[layer], scsh1, n_lat)
            proj = _matmul_w(h, ab_w_in_t, i, w_rows=True, n_cols=i1, name="ab_in_zxbc")
            dt_raw = _matmul_w(h, ab_w_in_t, i, w_rows=True, col0=i1, n_cols=i2 - i1, name="ab_in_dt")
            qk_w = jnp.concatenate([jnp.tile(ab_q_norm[i] * qk_scale, na_heads), jnp.tile(ab_k_norm[i], na_heads)])
            qkvb = _matmul_w(h, ab_w_in_t, i, w_rows=True, col0=i2, out_dtype=BF16, epilogue="qknorm", qk_w=qk_w,
                             name="ab_in_qkv")
            xbc = _conv_silu(proj, i0, ab_conv_w[i], ab_conv_b[i], n_lat)
            y = _ssd(xbc, dt_raw, ab_dt_bias[i], ab_a_log[i], n_lat, n_inner)
            dskip = jnp.repeat(ab_d_skip[i, 0] + ab_d_skip[i, 1], SSM_HEAD_DIM)
            gn = _gated_norm(y, xbc, proj, dskip, ab_norm_w[i])
            bias = _na_bias_tiles(ab_rpb[i], n_lat // GRID_W)
            al = _neighbourhood_attention(qkvb, bias, n_heads=na_heads, n_lat=n_lat, n_ctx=n_ctx)
            xr = _matmul_w(gn, ab_w_out, i, x2=al, rows=rows, tn=512, epilogue="resid", res=xr, gate=mod[:, 2, :],
                           n_lat=n_lat, name="mixer_out")
        else:
            p = _matmul_w(xr, c_w_qkv, i, norm=(norm1_w[layer], scsh1), n_lat=n_lat, name="c_in_qkv")
            wvec = jnp.concatenate([jnp.tile(c_q_norm[i] * qk_scale, attn_heads), jnp.tile(c_k_norm[i], kv_heads),
                                    jnp.ones((kv_heads * HEAD_DIM,), F32)])
            qk, vt = _prep(p, wvec, attn_heads + kv_heads, attn_heads + kv_heads, cos, sin, n_vt=kv_heads)
            att = _attention(qk, vt, n_heads=attn_heads, n_kv=kv_heads, n_lat=n_lat, q_rows=rows)
            xr = _matmul_w(att, c_w_out, i, rows=rows, epilogue="resid", res=xr, gate=mod[:, 2, :], n_lat=n_lat,
                           name="mixer_out")
        a = _matmul_w(xr, w_mlp_in, layer, norm=(norm2_w[layer], scsh2), out_dtype=BF16, epilogue="relu2",
                      n_lat=n_lat, name="mlp_in")
        xr = _matmul(a, w_mlp_out_b, layer, epilogue="resid", res=xr, gate=mod[:, 5, :], n_lat=n_lat,
                     name="mlp_out")
    return xr[None]
```

```python
import functools
import math

import numpy as np
import jax
import jax.numpy as jnp
from jax import lax
from jax.experimental import pallas as pl
from jax.experimental.pallas import tpu as pltpu

F32 = jnp.float32
BF16 = jnp.bfloat16

GRID_W = 64
HEAD_DIM = 128
ROPE_THETA = 10000.0
RMS_EPS = 1e-6
N_MOD = 6
SSM_HEAD_DIM = 64
SSM_GROUPS = 8
SSM_STATE = 128
SSM_NORM_GROUPS = 8
CONV_K = 5
NA_ROWS = 8
NA_COLS = 16

LANES = 128
SUBLANES = 8
VMEM_LIMIT_BYTES = 48 * 1024 * 1024
VMEM_LIMIT_WEIGHT_RESIDENT = 56 * 1024 * 1024

ROW_TILE = 256
SSD_CHUNK = 256
VT_ROWS = HEAD_DIM + 2 * SUBLANES
NA_QROWS = 4
NA_WROWS = 12
NA_HEADS_PER_STEP = 4
ATTN_Q_TILE = 512


def _pick(n, candidates):
    for c in candidates:
        if n % c == 0:
            return c
    raise ValueError(f"no tile in {candidates} divides {n}")


def _cparams(sem):
    return pltpu.CompilerParams(dimension_semantics=sem, vmem_limit_bytes=VMEM_LIMIT_BYTES)


def _sigmoid(v):
    return 1.0 / (1.0 + jnp.exp(-v))


def _softplus(v):
    return jnp.maximum(v, 0.0) + jnp.log(1.0 + jnp.exp(-jnp.abs(v)))


def _dot(a, b):
    return jnp.dot(a, b, preferred_element_type=F32)


def _dot_nt(a, b):
    return lax.dot_general(a, b, (((1,), (1,)), ((), ())), preferred_element_type=F32)


def _split3(x):
    hi = x.astype(BF16)
    r1 = x - hi.astype(F32)
    mid = r1.astype(BF16)
    lo = (r1 - mid.astype(F32)).astype(BF16)
    return hi, mid, lo


def _mod_kernel(v_ref, w_ref, b_ref, o_ref):
    v = v_ref[...]
    s = (v * _sigmoid(v)).astype(BF16)
    o_ref[0] = _dot(s, w_ref[0].astype(BF16)) + b_ref[0]


def _modulation(vecs, w_mod, b_mod):
    depth, d, n = w_mod.shape
    tn = _pick(n, (1024, 512, 256, 128))
    return pl.pallas_call(
        _mod_kernel,
        grid=(depth, n // tn),
        in_specs=[pl.BlockSpec((SUBLANES, d), lambda l, j: (0, 0)),
                  pl.BlockSpec((1, d, tn), lambda l, j: (l, 0, j)),
                  pl.BlockSpec((1, 1, tn), lambda l, j: (l, 0, j))],
        out_specs=pl.BlockSpec((1, SUBLANES, tn), lambda l, j: (l, 0, j)),
        out_shape=jax.ShapeDtypeStruct((depth, SUBLANES, n), F32),
        compiler_params=_cparams(("arbitrary", "arbitrary")),
        name="modulation",
    )(vecs, w_mod, b_mod.reshape(depth, 1, n))


def _normmod_kernel(x_ref, w_ref, m_ref, o_ref):
    x = x_ref[...]
    y = x * lax.rsqrt(jnp.mean(x * x, axis=-1, keepdims=True) + RMS_EPS)
    m = m_ref[0]
    o_ref[...] = ((y * w_ref[...]) * (1.0 + m[0:1, :]) + m[1:2, :]).astype(o_ref.dtype)


def _normmod(x, w, scsh, n_lat):
    r, d = x.shape
    nlb = n_lat // ROW_TILE
    return pl.pallas_call(
        _normmod_kernel,
        grid=(r // ROW_TILE,),
        in_specs=[pl.BlockSpec((ROW_TILE, d), lambda i: (i, 0)),
                  pl.BlockSpec((1, d), lambda i: (0, 0)),
                  pl.BlockSpec((1, 2, d), lambda i: (i // nlb, 0, 0))],
        out_specs=pl.BlockSpec((ROW_TILE, d), lambda i: (i, 0)),
        out_shape=jax.ShapeDtypeStruct((r, d), BF16),
        compiler_params=_cparams(("arbitrary",)),
        name="normmod",
    )(x, w.reshape(1, d), scsh)


def _mm_kernel(*refs, nk, nk1, two_lhs, epilogue, n_lat, tm):
    refs = list(refs)
    x_ref = refs.pop(0)
    x2_ref = refs.pop(0) if two_lhs else None
    w_ref = refs.pop(0)
    if epilogue == "resid":
        res_ref = refs.pop(0)
        gate_ref = refs.pop(0)
    o_ref = refs.pop(0)

    def finish(acc):
        if epilogue == "relu2":
            a = jnp.maximum(acc, 0.0)
            o_ref[...] = (a * a).astype(o_ref.dtype)
        elif epilogue == "resid":
            row = pl.program_id(0) * tm + lax.broadcasted_iota(jnp.int32, (tm, 1), 0)
            g = jnp.where(row < n_lat, gate_ref[0:1, :], gate_ref[1:2, :])
            o_ref[...] = res_ref[...] + g * acc
        else:
            o_ref[...] = acc.astype(o_ref.dtype)

    if nk == 1:
        finish(_dot(x_ref[...], w_ref[0]))
        return
    acc_ref, = refs
    k = pl.program_id(2)

    @pl.when(k == 0)
    def _():
        acc_ref[...] = jnp.zeros_like(acc_ref)

    if two_lhs:
        @pl.when(k < nk1)
        def _():
            acc_ref[...] += _dot(x_ref[...], w_ref[0])

        @pl.when(k >= nk1)
        def _():
            acc_ref[...] += _dot(x2_ref[...], w_ref[0])
    else:
        acc_ref[...] += _dot(x_ref[...], w_ref[0])

    @pl.when(k == nk - 1)
    def _():
        finish(acc_ref[...])


def _matmul(x, w, layer, *, x2=None, rows=None, n_cols=None, out_dtype=F32, epilogue="none", res=None, gate=None,
            n_lat=0, name="matmul"):
    r = x.shape[0] if rows is None else rows
    _, kdim, n_w = w.shape
    n = n_w if n_cols is None else n_cols
    k1 = x.shape[1]
    tm = _pick(r, (1024, 768, 512, 256))
    tn = _pick(n, (1024, 512, 256, 128))
    if x2 is None:
        tk = kdim if kdim <= 2048 else _pick(kdim, (2048, 1024, 512))
    else:
        tk = math.gcd(k1, x2.shape[1])
    nk, nk1 = kdim // tk, k1 // tk
    in_specs = [pl.BlockSpec((tm, tk), lambda i, j, k: (i, jnp.minimum(k, nk1 - 1)))]
    args = [x]
    if x2 is not None:
        in_specs.append(pl.BlockSpec((tm, tk), lambda i, j, k: (i, jnp.maximum(k - nk1, 0))))
        args.append(x2)
    in_specs.append(pl.BlockSpec((1, tk, tn), lambda i, j, k: (layer, k, j)))
    args.append(w)
    if epilogue == "resid":
        in_specs += [pl.BlockSpec((tm, tn), lambda i, j, k: (i, j)),
                     pl.BlockSpec((2, tn), lambda i, j, k: (0, j))]
        args += [res, gate]
    kernel = functools.partial(_mm_kernel, nk=nk, nk1=nk1, two_lhs=x2 is not None, epilogue=epilogue, n_lat=n_lat,
                               tm=tm)
    return pl.pallas_call(
        kernel,
        grid=(r // tm, n // tn, nk),
        in_specs=in_specs,
        out_specs=pl.BlockSpec((tm, tn), lambda i, j, k: (i, j)),
        out_shape=jax.ShapeDtypeStruct((r, n), out_dtype),
        scratch_shapes=[] if nk == 1 else [pltpu.VMEM((tm, tn), F32)],
        compiler_params=_cparams(("arbitrary", "arbitrary", "arbitrary")),
        name=name,
    )(*args)


def _mmw_kernel(*refs, prologue, two_lhs, w_rows, epilogue, n_lat, tm, k1, qk_tiles):
    refs = list(refs)
    x_ref = refs.pop(0)
    if prologue:
        nw_ref = refs.pop(0)
        mod_ref = refs.pop(0)
    x2_ref = refs.pop(0) if two_lhs else None
    w_ref = refs.pop(0)
    if epilogue == "resid":
        res_ref = refs.pop(0)
        gate_ref = refs.pop(0)
    if epilogue == "qknorm":
        qkw_ref = refs.pop(0)
    o_ref, wb_ref = refs[:2]
    j = pl.program_id(0)
    i = pl.program_id(1)

    @pl.when(i == 0)
    def _():
        wb_ref[...] = w_ref[0].astype(BF16)

    if prologue:
        h_ref = refs[2]
        for s in range(tm // ROW_TILE):
            rs = slice(s * ROW_TILE, (s + 1) * ROW_TILE)
            ctx_rows = i * tm + s * ROW_TILE >= n_lat
            scale = nw_ref[...] * (1.0 + jnp.where(ctx_rows, mod_ref[1, 0:1, :], mod_ref[0, 0:1, :]))
            shift = jnp.where(ctx_rows, mod_ref[1, 1:2, :], mod_ref[0, 1:2, :])
            x = x_ref[rs, :]
            y = x * lax.rsqrt(jnp.mean(x * x, axis=-1, keepdims=True) + RMS_EPS)
            h_ref[rs, :] = (y * scale + shift).astype(BF16)
        h = h_ref[...]
    else:
        h = x_ref[...]
    if two_lhs:
        acc = _dot(h, wb_ref[0:k1, :]) + _dot(x2_ref[...], wb_ref[k1:, :])
    elif w_rows:
        acc = _dot_nt(h, wb_ref[...])
    else:
        acc = _dot(h, wb_ref[...])
    if epilogue == "relu2":
        a = jnp.maximum(acc, 0.0)
        o_ref[...] = (a * a).astype(o_ref.dtype)
    elif epilogue == "resid":
        is_lat = (i * tm + lax.broadcasted_iota(jnp.int32, (tm, 1), 0)) < n_lat
        o_ref[...] = res_ref[...] + jnp.where(is_lat, gate_ref[0:1, :], gate_ref[1:2, :]) * acc
    elif epilogue == "qknorm":
        @pl.when(j < qk_tiles)
        def _():
            for b in range(acc.shape[1] // HEAD_DIM):
                sl = slice(b * HEAD_DIM, (b + 1) * HEAD_DIM)
                u = acc[:, sl]
                u = u * lax.rsqrt(jnp.mean(u * u, axis=-1, keepdims=True) + RMS_EPS) * qkw_ref[:, sl]
                o_ref[:, sl] = u.astype(o_ref.dtype)

        @pl.when(j >= qk_tiles)
        def _():
            o_ref[...] = acc.astype(o_ref.dtype)
    else:
        o_ref[...] = acc.astype(o_ref.dtype)


def _matmul_w(x, w, layer, *, norm=None, x2=None, rows=None, w_rows=False, col0=0, n_cols=None, tn=1024, out_dtype=F32,
              epilogue="none", res=None, gate=None, qk_w=None, n_lat=0, name="matmul_w"):
    r = x.shape[0] if rows is None else rows
    kdim, n_w = (w.shape[2], w.shape[1]) if w_rows else (w.shape[1], w.shape[2])
    n = n_w - col0 if n_cols is None else n_cols
    tn = min(tn, n)
    k1 = x.shape[1]
    tm = _pick(r, (1024, 768, 512, 256))
    in_specs = [pl.BlockSpec((tm, k1), lambda j, i: (i, 0))]
    args = [x]
    scratch = [pltpu.VMEM((tn, kdim) if w_rows else (kdim, tn), BF16)]
    if norm is not None:
        norm_w, scsh = norm
        in_specs += [pl.BlockSpec((1, k1), lambda j, i: (0, 0)),
                     pl.BlockSpec((2, 2, k1), lambda j, i: (0, 0, 0))]
        args += [norm_w.reshape(1, k1), scsh]
        scratch.append(pltpu.VMEM((tm, k1), BF16))
    if x2 is not None:
        in_specs.append(pl.BlockSpec((tm, x2.shape[1]), lambda j, i: (i, 0)))
        args.append(x2)
    if w_rows:
        in_specs.append(pl.BlockSpec((pl.Element(1), pl.Element(tn), pl.Element(kdim)),
                                     lambda j, i: (layer, pl.multiple_of(col0 + j * tn, SUBLANES), 0)))
    else:
        assert col0 == 0
        in_specs.append(pl.BlockSpec((1, kdim, tn), lambda j, i: (layer, 0, j)))
    args.append(w)
    if epilogue == "resid":
        in_specs += [pl.BlockSpec((tm, tn), lambda j, i: (i, j)),
                     pl.BlockSpec((2, tn), lambda j, i: (0, j))]
        args += [res, gate]
    qk_tiles = 0
    if epilogue == "qknorm":
        qk_tiles = qk_w.shape[0] // tn
        assert qk_w.shape[0] % tn == 0
        in_specs.append(pl.BlockSpec((1, tn), lambda j, i: (0, jnp.minimum(j, qk_tiles - 1))))
        args.append(qk_w.reshape(1, -1))
    kernel = functools.partial(_mmw_kernel, prologue=norm is not None, two_lhs=x2 is not None, w_rows=w_rows,
                               epilogue=epilogue, n_lat=n_lat, tm=tm, k1=k1, qk_tiles=qk_tiles)
    return pl.pallas_call(
        kernel,
        grid=(n // tn, r // tm),
        in_specs=in_specs,
        out_specs=pl.BlockSpec((tm, tn), lambda j, i: (i, j)),
        out_shape=jax.ShapeDtypeStruct((r, n), out_dtype),
        scratch_shapes=scratch,
        compiler_params=pltpu.CompilerParams(dimension_semantics=("arbitrary", "arbitrary"),
                                             vmem_limit_bytes=VMEM_LIMIT_WEIGHT_RESIDENT),
        name=name,
    )(*args)


def _prep_kernel(*refs, n_norm, n_rope, n_blocks, n_vt):
    refs = list(refs)
    p_ref, w_ref = refs[:2]
    if n_rope:
        cos = refs[2][...]
        sin = refs[3][...]
        even = (lax.broadcasted_iota(jnp.int32, cos.shape, 1) % 2) == 0
    o_ref = refs[-2] if n_vt else refs[-1]
    for b in range(n_blocks):
        sl = slice(b * HEAD_DIM, (b + 1) * HEAD_DIM)
        u = p_ref[:, sl]
        if b < n_norm:
            u = u * lax.rsqrt(jnp.mean(u * u, axis=-1, keepdims=True) + RMS_EPS) * w_ref[:, sl]
        if b < n_rope:
            partner = jnp.where(even, pltpu.roll(u, HEAD_DIM - 1, 1), pltpu.roll(u, 1, 1))
            u = u * cos + partner * sin
        if b < n_blocks - n_vt:
            o_ref[:, sl] = u.astype(o_ref.dtype)
        else:
            vt_ref = refs[-1]
            hv = b - (n_blocks - n_vt)
            vt_ref[hv, 0, 0:HEAD_DIM, :] = u.T.astype(BF16)
            pad_row = lax.broadcasted_iota(jnp.int32, (VT_ROWS - HEAD_DIM, u.shape[0]), 0)
            vt_ref[hv, 0, HEAD_DIM:, :] = jnp.where(pad_row == 0, 1.0, 0.0).astype(BF16)


def _attn_key_tile(n_all):
    for tk in (768, ROW_TILE):
        if n_all % tk == 0 and (n_all // tk) % 2 == 1:
            return tk
    raise ValueError(f"no key tile for {n_all} rows")


def _prep(p, wvec, n_norm, n_rope=0, cos=None, sin=None, n_vt=0):
    r, n = p.shape
    n_blocks = n // HEAD_DIM
    n_out = (n_blocks - n_vt) * HEAD_DIM
    tk = _attn_key_tile(r) if n_vt else ROW_TILE
    per = tk // ROW_TILE
    in_specs = [pl.BlockSpec((ROW_TILE, n), lambda i: (i, 0)),
                pl.BlockSpec((1, n), lambda i: (0, 0))]
    args = [p, wvec.reshape(1, n)]
    if n_rope:
        in_specs += [pl.BlockSpec((ROW_TILE, HEAD_DIM), lambda i: (i, 0))] * 2
        args += [cos, sin]
    out_specs = [pl.BlockSpec((ROW_TILE, n_out), lambda i: (i, 0))]
    out_shape = [jax.ShapeDtypeStruct((r, n_out), BF16)]
    if n_vt:
        out_specs.append(pl.BlockSpec((n_vt, 1, VT_ROWS, ROW_TILE), lambda i: (0, i // per, 0, i % per)))
        out_shape.append(jax.ShapeDtypeStruct((n_vt, r // tk, VT_ROWS, tk), BF16))
    out = pl.pallas_call(
        functools.partial(_prep_kernel, n_norm=n_norm, n_rope=n_rope, n_blocks=n_blocks, n_vt=n_vt),
        grid=(r // ROW_TILE,),
        in_specs=in_specs,
        out_specs=out_specs,
        out_shape=out_shape,
        compiler_params=_cparams(("arbitrary",)),
        name="qkv_prep",
    )(*args)
    return out if n_vt else out[0]


def _attn_kernel(q_ref, k_ref, vt_ref, o_ref, s0_ref, s1_ref, m_ref, acc_ref, *, group, tq, tk, n_lat, n_all):
    n_tiles = n_all // tk
    n_ctx = n_all - n_lat
    m_ref[...] = jnp.full(m_ref.shape, -jnp.inf, F32)
    acc_ref[...] = jnp.zeros(acc_ref.shape, F32)

    def scores(kc, s_ref, g):
        st = _dot_nt(kc, q_ref[:, g * HEAD_DIM:(g + 1) * HEAD_DIM])
        s_ref[g, 0:kc.shape[0], :] = st
        s_ref[g, tk:tk + 1, :] = jnp.max(st, axis=0, keepdims=True)

    def consume(vtc, s_ref, g):
        st = s_ref[g, 0:vtc.shape[1], :]
        m_prev = m_ref[g]
        m_new = jnp.maximum(m_prev, s_ref[g, tk:tk + 1, :])
        p = jnp.exp2(st - m_new).astype(BF16)
        acc_ref[g] = jnp.exp2(m_prev - m_new) * acc_ref[g] + _dot(vtc, p)
        m_ref[g] = m_new

    def key_tile(c):
        return k_ref[pl.ds(pl.multiple_of(c * tk, tk), tk), :]

    def step(k_next, vt_cur, s_next, s_cur):
        for g in range(group):
            if k_next is not None:
                scores(k_next, s_next, g)
            if vt_cur is not None:
                consume(vt_cur, s_cur, g)

    def pair(i, carry):
        c = 2 * i
        step(key_tile(c + 1), vt_ref[0, c], s1_ref, s0_ref)
        step(key_tile(c + 2), vt_ref[0, c + 1], s0_ref, s1_ref)
        return carry

    @pl.when(pl.program_id(1) < n_lat // tq)
    def _():
        step(key_tile(0), None, s0_ref, None)
        lax.fori_loop(0, (n_tiles - 1) // 2, pair, 0)
        step(None, vt_ref[0, n_tiles - 1], None, s0_ref)

    @pl.when(pl.program_id(1) >= n_lat // tq)
    def _():
        step(k_ref[n_lat:n_all, :], None, s0_ref, None)
        step(None, vt_ref[0, n_tiles - 1, :, tk - n_ctx:tk], None, s0_ref)

    for g in range(group):
        out = acc_ref[g, 0:HEAD_DIM, :] / acc_ref[g, HEAD_DIM:HEAD_DIM + 1, :]
        o_ref[:, g * HEAD_DIM:(g + 1) * HEAD_DIM] = out.T.astype(o_ref.dtype)


def _attention(qk, vt, *, n_heads, n_kv, n_lat, q_rows):
    n_all = qk.shape[0]
    group = n_heads // n_kv
    tq = ATTN_Q_TILE
    tk = vt.shape[-1]
    gw = group * HEAD_DIM
    assert n_all - n_lat <= min(tk, tq) and n_lat % tq == 0
    kernel = functools.partial(_attn_kernel, group=group, tq=tq, tk=tk, n_lat=n_lat, n_all=n_all)
    return pl.pallas_call(
        kernel,
        grid=(n_kv, pl.cdiv(q_rows, tq)),
        in_specs=[pl.BlockSpec((tq, gw), lambda h, i: (i, h)),
                  pl.BlockSpec((n_all, HEAD_DIM), lambda h, i: (0, n_heads + h)),
                  pl.BlockSpec((1, n_all // tk, VT_ROWS, tk), lambda h, i: (h, 0, 0, 0))],
        out_specs=pl.BlockSpec((tq, gw), lambda h, i: (i, h)),
        out_shape=jax.ShapeDtypeStruct((q_rows, n_heads * HEAD_DIM), BF16),
        scratch_shapes=[pltpu.VMEM((group, tk + SUBLANES, tq), F32),
                        pltpu.VMEM((group, tk + SUBLANES, tq), F32),
                        pltpu.VMEM((group, 1, tq), F32),
                        pltpu.VMEM((group, VT_ROWS, tq), F32)],
        compiler_params=_cparams(("arbitrary", "arbitrary")),
        name="gqa_attention",
    )(qk, qk, vt)


def _na_kernel(q_ref, k_ref, v_ref, bias_ref, o_ref, *, n_lat, rows, n_ctx):
    nb = rows // NA_QROWS
    b = pl.program_id(1)
    heads = [slice(j * HEAD_DIM, (j + 1) * HEAD_DIM) for j in range(NA_HEADS_PER_STEP)]
    s_c = [_dot_nt(q_ref[:, hs], k_ref[n_lat:n_lat + n_ctx, hs]) for hs in heads]

    @pl.when(b < nb)
    def _():
        w0 = jnp.clip(NA_QROWS * b - NA_ROWS // 2, 0, rows - NA_WROWS)
        win = pl.ds(pl.multiple_of(w0 * GRID_W, GRID_W), NA_WROWS * GRID_W)
        s_w = [_dot_nt(q_ref[:, hs], k_ref[win, hs]) + bias_ref[0, j] for j, hs in enumerate(heads)]
        m = [jnp.maximum(jnp.max(sw, axis=1, keepdims=True), jnp.max(sc, axis=1, keepdims=True))
             for sw, sc in zip(s_w, s_c)]
        p_w = [jnp.exp2(sw - mj) for sw, mj in zip(s_w, m)]
        p_c = [jnp.exp2(sc - mj) for sc, mj in zip(s_c, m)]
        for j, hs in enumerate(heads):
            l = jnp.sum(p_w[j], axis=1, keepdims=True) + jnp.sum(p_c[j], axis=1, keepdims=True)
            o = _dot(p_w[j].astype(BF16), v_ref[win, hs]) + _dot(p_c[j].astype(BF16), v_ref[n_lat:n_lat + n_ctx, hs])
            o_ref[:, hs] = (o / l).astype(o_ref.dtype)

    @pl.when(b >= nb)
    def _():
        for j, hs in enumerate(heads):
            p_c = jnp.exp2(s_c[j] - jnp.max(s_c[j], axis=1, keepdims=True))
            l = jnp.sum(p_c, axis=1, keepdims=True)
            o_ref[:, hs] = (_dot(p_c.astype(BF16), v_ref[n_lat:n_lat + n_ctx, hs]) / l).astype(o_ref.dtype)


def _na_bias_tiles(rpb, rows):
    h = rpb.shape[0]
    col = np.arange(GRID_W)
    c0 = np.clip(col - NA_COLS // 2, 0, GRID_W - NA_COLS)
    col_ok = (col[None, :] >= c0[:, None]) & (col[None, :] < c0[:, None] + NA_COLS)
    pad = GRID_W - NA_COLS
    rp = jnp.pad(rpb * math.log2(math.e), ((0, 0), (0, 0), (pad, pad)))
    toep = jnp.stack([rp[:, :, NA_COLS - 1 - q + pad:NA_COLS - 1 - q + pad + GRID_W] for q in range(GRID_W)], axis=2)
    toep = jnp.where(col_ok[None, None], toep, -jnp.inf)
    neg = jnp.full((h, GRID_W, GRID_W), -jnp.inf, F32)
    half = NA_ROWS // 2
    kinds = (lambda j: (j, 0), lambda j: (half + j, j), lambda j: (NA_ROWS + j, half))
    tiles = []
    for kind in kinds:
        qrows = []
        for j in range(NA_QROWS):
            q_row, win = kind(j)
            blocks = []
            for ik in range(NA_WROWS):
                dr = ik - q_row + (NA_ROWS - 1)
                ok = win <= ik < win + NA_ROWS
                blocks.append(toep[:, dr] if ok else neg)
            qrows.append(jnp.concatenate(blocks, axis=-1))
        tiles.append(jnp.concatenate(qrows, axis=-2))
    return jnp.stack(tiles, axis=0)


def _neighbourhood_attention(qkv, bias, *, n_heads, n_lat, n_ctx):
    r = qkv.shape[0]
    rows = n_lat // GRID_W
    nb = rows // NA_QROWS
    tq = NA_QROWS * GRID_W
    assert tq == n_ctx and rows >= NA_WROWS + NA_QROWS

    def bias_map(h, b):
        return (jnp.where(b == 0, 0, jnp.where(b >= nb - 1, 2, 1)), h, 0, 0)

    kernel = functools.partial(_na_kernel, n_lat=n_lat, rows=rows, n_ctx=n_ctx)
    hps = NA_HEADS_PER_STEP
    hw = hps * HEAD_DIM
    n_steps = n_heads // hps
    return pl.pallas_call(
        kernel,
        grid=(n_steps, r // tq),
        in_specs=[pl.BlockSpec((tq, hw), lambda h, b: (b, h)),
                  pl.BlockSpec((r, hw), lambda h, b: (0, n_steps + h)),
                  pl.BlockSpec((r, hw), lambda h, b: (0, 2 * n_steps + h)),
                  pl.BlockSpec((1, hps, tq, NA_WROWS * GRID_W), bias_map)],
        out_specs=pl.BlockSpec((tq, hw), lambda h, b: (b, h)),
        out_shape=jax.ShapeDtypeStruct((r, n_heads * HEAD_DIM), BF16),
        compiler_params=_cparams(("arbitrary", "arbitrary")),
        name="neighbourhood_attention",
    )(qkv, qkv, qkv, bias)


def _conv_kernel(u_ref, prev_ref, next_ref, w_ref, b_ref, o_ref, ext_ref, *, n_lat):
    tr = u_ref.shape[0]
    i = pl.program_id(0)
    nlb = n_lat // tr
    has_prev = jnp.logical_and(i != 0, i != nlb)
    has_next = i < nlb - 1
    ext_ref[0:SUBLANES, :] = jnp.where(has_prev, prev_ref[...], 0.0)
    ext_ref[SUBLANES:SUBLANES + tr, :] = u_ref[...]
    ext_ref[SUBLANES + tr:, :] = jnp.where(has_next, next_ref[...], 0.0)
    ext = ext_ref[...]
    n_ext = ext.shape[0]
    acc = jnp.broadcast_to(b_ref[...], o_ref.shape)
    for j in range(CONV_K):
        d = j - CONV_K // 2
        shifted = ext if d == 0 else pltpu.roll(ext, (n_ext - d) % n_ext, 0)
        acc = acc + w_ref[j:j + 1, :] * shifted[SUBLANES:SUBLANES + tr, :]
    o_ref[...] = acc * _sigmoid(acc)


def _conv_silu(proj, col0, conv_w, conv_b, n_lat):
    r = proj.shape[0]
    ch = conv_w.shape[1]
    tr = ROW_TILE
    tc = _pick(ch, (1024, 512, 256, 128))
    cb0 = col0 // tc
    assert col0 % tc == 0
    rb = tr // SUBLANES
    last = r // SUBLANES - 1
    return pl.pallas_call(
        functools.partial(_conv_kernel, n_lat=n_lat),
        grid=(r // tr, ch // tc),
        in_specs=[pl.BlockSpec((tr, tc), lambda i, j: (i, cb0 + j)),
                  pl.BlockSpec((SUBLANES, tc), lambda i, j: (jnp.maximum(i * rb - 1, 0), cb0 + j)),
                  pl.BlockSpec((SUBLANES, tc), lambda i, j: (jnp.minimum((i + 1) * rb, last), cb0 + j)),
                  pl.BlockSpec((CONV_K, tc), lambda i, j: (0, j)),
                  pl.BlockSpec((1, tc), lambda i, j: (0, j))],
        out_specs=pl.BlockSpec((tr, tc), lambda i, j: (i, j)),
        out_shape=jax.ShapeDtypeStruct((r, ch), F32),
        scratch_shapes=[pltpu.VMEM((tr + 2 * SUBLANES, tc), F32)],
        compiler_params=_cparams(("arbitrary", "arbitrary")),
        name="conv_silu",
    )(proj, proj, proj, conv_w, conv_b.reshape(1, ch))


def _ssd_kernel(xs_ref, b_ref, c_ref, dt_ref, dtt_ref, bias_ref, biast_ref, alog_ref, alogt_ref, e_ref,
                y_ref, st_ref, *, n_groups, heads_per_group):
    t = xs_ref.shape[0]
    d = pl.program_id(0)
    c = pl.program_id(1)
    gw = heads_per_group * SSM_HEAD_DIM

    @pl.when(c == 0)
    def _():
        st_ref[...] = jnp.zeros(st_ref.shape, F32)

    dt = _softplus(dt_ref[0] + bias_ref[0])
    dtt = _softplus(dtt_ref[0] + biast_ref[0])
    da = dt * (-jnp.exp(alog_ref[0]))
    dat = dtt * (-jnp.exp(alogt_ref[0]))
    row = lax.broadcasted_iota(jnp.int32, (t, t), 0)
    col = lax.broadcasted_iota(jnp.int32, (t, t), 1)
    lag = jnp.where(d == 0, row - col, col - row)
    valid = lag >= 0
    ones_v = jnp.where(valid, 1.0, 0.0).astype(BF16)
    ones_vt = jnp.where(lag <= 0, 1.0, 0.0).astype(BF16)
    cum = sum(_dot(ones_v, piece) for piece in _split3(da))
    cumt = sum(_dot(piece, ones_vt) for piece in _split3(dat))
    tot = jnp.sum(da, axis=0, keepdims=True)
    log2e = math.log2(math.e)
    cum_l2 = cum * log2e
    cumt_s2 = (cumt - jnp.log(dtt)) * log2e
    unreached = jnp.where(valid, 0.0, -jnp.inf)
    stack = jnp.concatenate([dt * jnp.exp(tot - cum), jnp.exp(cum),
                             jnp.broadcast_to(jnp.exp(tot), (SUBLANES, tot.shape[1]))], axis=0)
    wide = _dot(jnp.concatenate(_split3(stack), axis=1), e_ref[...])
    win_w = wide[0:t]
    ecum_w = wide[t:2 * t]
    cdec_w = wide[2 * t:2 * t + 1]
    xs = xs_ref[...]
    xin = (xs * win_w).astype(BF16)
    lane_head = lax.broadcasted_iota(jnp.int32, (t, gw), 1) // SSM_HEAD_DIM

    for g in range(n_groups):
        ns = slice(g * SSM_STATE, (g + 1) * SSM_STATE)
        hs = slice(g * gw, (g + 1) * gw)
        bg = b_ref[:, ns]
        cg = c_ref[:, ns].astype(BF16)
        cb = _dot_nt(cg, bg.astype(BF16))
        acc = _dot(cg, st_ref[g].astype(BF16)) * ecum_w[:, hs]
        xs_g = xs[:, hs].astype(BF16)
        for k in range(heads_per_group):
            h = g * heads_per_group + k
            decay = jnp.exp2((cum_l2[:, h:h + 1] - cumt_s2[h:h + 1, :]) + unreached)
            mh = (cb * decay).astype(BF16)
            xk = jnp.where(lane_head == k, xs_g, jnp.zeros_like(xs_g))
            acc = acc + _dot(mh, xk)
        y_ref[0, :, hs] = acc
        st_ref[g] = st_ref[g] * cdec_w[:, hs] + _dot(bg.T.astype(BF16), xin[:, hs])


def _ssd(xbc, dt_raw, dt_bias, a_log, n_lat, n_inner):
    r = xbc.shape[0]
    t = SSD_CHUNK
    heads = n_inner // SSM_HEAD_DIM
    gn = SSM_GROUPS * SSM_STATE
    hpg = heads // SSM_GROUPS
    n_lc = n_lat // t
    n_cc = (r - n_lat) // t
    assert (r - n_lat) % t == 0 and n_inner % gn == 0
    dt3 = dt_raw[:, :2 * heads].reshape(r, 2, heads).transpose(1, 0, 2)
    dt3t = dt3.transpose(0, 2, 1)
    expand = jnp.asarray(np.tile(np.repeat(np.eye(heads, dtype=np.float32), SSM_HEAD_DIM, axis=1), (3, 1)), BF16)

    def chunk(d, c):
        fwd = jnp.where(c < n_cc, n_lc + c, c - n_cc)
        bwd = n_lc + n_cc - 1 - c
        return jnp.where(d == 0, fwd, bwd)

    xb = n_inner // gn
    kernel = functools.partial(_ssd_kernel, n_groups=SSM_GROUPS, heads_per_group=hpg)
    return pl.pallas_call(
        kernel,
        grid=(2, n_lc + n_cc),
        in_specs=[pl.BlockSpec((t, n_inner), lambda d, c: (chunk(d, c), 0)),
                  pl.BlockSpec((t, gn), lambda d, c: (chunk(d, c), xb)),
                  pl.BlockSpec((t, gn), lambda d, c: (chunk(d, c), xb + 1)),
                  pl.BlockSpec((1, t, heads), lambda d, c: (d, chunk(d, c), 0)),
                  pl.BlockSpec((1, heads, t), lambda d, c: (d, 0, chunk(d, c))),
                  pl.BlockSpec((1, 1, heads), lambda d, c: (d, 0, 0)),
                  pl.BlockSpec((1, heads, 1), lambda d, c: (d, 0, 0)),
                  pl.BlockSpec((1, 1, heads), lambda d, c: (d, 0, 0)),
                  pl.BlockSpec((1, heads, 1), lambda d, c: (d, 0, 0)),
                  pl.BlockSpec((3 * heads, n_inner), lambda d, c: (0, 0))],
        out_specs=pl.BlockSpec((1, t, n_inner), lambda d, c: (d, chunk(d, c), 0)),
        out_shape=jax.ShapeDtypeStruct((2, r, n_inner), F32),
        scratch_shapes=[pltpu.VMEM((SSM_GROUPS, SSM_STATE, hpg * SSM_HEAD_DIM), F32)],
        compiler_params=_cparams(("arbitrary", "arbitrary")),
        name="ssd_scan",
    )(xbc, xbc, xbc, dt3, dt3t, dt_bias.reshape(2, 1, heads), dt_bias.reshape(2, heads, 1),
      a_log.reshape(2, 1, heads), a_log.reshape(2, heads, 1), expand)


def _gated_norm_kernel(y_ref, xs_ref, z_ref, dskip_ref, w_ref, o_ref, *, n_groups):
    z = z_ref[...]
    u = (y_ref[0] + y_ref[1] + dskip_ref[...] * xs_ref[...]) * (z * _sigmoid(z))
    gw = u.shape[1] // n_groups
    for g in range(n_groups):
        sl = slice(g * gw, (g + 1) * gw)
        ug = u[:, sl]
        ug = ug * lax.rsqrt(jnp.mean(ug * ug, axis=-1, keepdims=True) + RMS_EPS)
        o_ref[:, sl] = (ug * w_ref[:, sl]).astype(o_ref.dtype)


def _gated_norm(y, xbc, proj, dskip_w, norm_w):
    _, r, n = y.shape
    return pl.pallas_call(
        functools.partial(_gated_norm_kernel, n_groups=SSM_NORM_GROUPS),
        grid=(r // ROW_TILE,),
        in_specs=[pl.BlockSpec((2, ROW_TILE, n), lambda i: (0, i, 0)),
                  pl.BlockSpec((ROW_TILE, n), lambda i: (i, 0)),
                  pl.BlockSpec((ROW_TILE, n), lambda i: (i, 0)),
                  pl.BlockSpec((1, n), lambda i: (0, 0)),
                  pl.BlockSpec((1, n), lambda i: (0, 0))],
        out_specs=pl.BlockSpec((ROW_TILE, n), lambda i: (i, 0)),
        out_shape=jax.ShapeDtypeStruct((r, n), BF16),
        compiler_params=_cparams(("arbitrary",)),
        name="gated_norm",
    )(y, xbc, proj, dskip_w.reshape(1, n), norm_w.reshape(1, n))


def _rope_tables(n_lat, n_ctx):
    t = jnp.arange(n_lat)
    half = HEAD_DIM // 2
    inv = 1.0 / (ROPE_THETA ** (jnp.arange(0, half, 2, dtype=F32) / half))
    ang = jnp.concatenate([(t // GRID_W).astype(F32)[:, None] * inv,
                           (t % GRID_W).astype(F32)[:, None] * inv], axis=-1)
    cos = jnp.repeat(jnp.cos(ang), 2, axis=-1)
    sin = jnp.repeat(jnp.sin(ang), 2, axis=-1) * jnp.asarray(np.tile(np.array([-1.0, 1.0], np.float32), half))
    cos = jnp.concatenate([cos, jnp.ones((n_ctx, HEAD_DIM), F32)], axis=0)
    sin = jnp.concatenate([sin, jnp.zeros((n_ctx, HEAD_DIM), F32)], axis=0)
    return cos, sin


def kernel(x, c, ctx, c_ctx, w_mod, b_mod, norm1_w, norm2_w, w_mlp_in, w_mlp_out, ab_w_in, ab_conv_w, ab_conv_b,
           ab_dt_bias, ab_a_log, ab_d_skip, ab_norm_w, ab_q_norm, ab_k_norm, ab_rpb, ab_w_out, c_w_qkv, c_q_norm,
           c_k_norm, c_w_out):
    _, n_lat, d = x.shape
    n_ctx = ctx.shape[1]
    depth = w_mod.shape[0]
    assert x.shape[0] == 1 and n_ctx == ROW_TILE and n_lat % 1024 == 0
    n_inner = d
    ssm_heads = n_inner // SSM_HEAD_DIM
    conv_dim = ab_conv_w.shape[-1]
    na_heads = ab_rpb.shape[1]
    attn_heads = c_w_out.shape[1] // HEAD_DIM
    kv_heads = (c_w_qkv.shape[-1] // HEAD_DIM - attn_heads) // 2
    qk_scale = HEAD_DIM ** -0.5 * math.log2(math.e)
    i0 = n_inner
    i1 = i0 + conv_dim
    i2 = i1 + 2 * ssm_heads

    w_mlp_out_b = w_mlp_out.astype(BF16)
    ab_w_in_t = jnp.swapaxes(ab_w_in, 1, 2)

    xr = jnp.concatenate([x[0], ctx[0]], axis=0)
    vecs = jnp.zeros((SUBLANES, d), F32).at[0].set(c[0]).at[1].set(c_ctx)
    mods = _modulation(vecs, w_mod, b_mod)[:, :2].reshape(depth, 2, N_MOD, d)
    cos, sin = _rope_tables(n_lat, n_ctx)

    for layer in range(depth):
        last = layer == depth - 1
        i = layer // 2
        mod = mods[layer]
        scsh1 = jnp.stack([mod[:, 1], mod[:, 0]], axis=1)
        scsh2 = jnp.stack([mod[:, 4], mod[:, 3]], axis=1)
        rows = n_lat if last else n_lat + n_ctx
        if layer % 2 == 0:
            h = _normmod(xr, norm1_w[layer], scsh1, n_lat)
            proj = _matmul_w(h, ab_w_in_t, i, w_rows=True, n_cols=i1, name="ab_in_zxbc")
            dt_raw = _matmul_w(h, ab_w_in_t, i, w_rows=True, col0=i1, n_cols=i2 - i1, name="ab_in_dt")
            qk_w = jnp.concatenate([jnp.tile(ab_q_norm[i] * qk_scale, na_heads), jnp.tile(ab_k_norm[i], na_heads)])
            qkvb = _matmul_w(h, ab_w_in_t, i, w_rows=True, col0=i2, out_dtype=BF16, epilogue="qknorm", qk_w=qk_w,
                             name="ab_in_qkv")
            xbc = _conv_silu(proj, i0, ab_conv_w[i], ab_conv_b[i], n_lat)
            y = _ssd(xbc, dt_raw, ab_dt_bias[i], ab_a_log[i], n_lat, n_inner)
            dskip = jnp.repeat(ab_d_skip[i, 0] + ab_d_skip[i, 1], SSM_HEAD_DIM)
            gn = _gated_norm(y, xbc, proj, dskip, ab_norm_w[i])
            bias = _na_bias_tiles(ab_rpb[i], n_lat // GRID_W)
            al = _neighbourhood_attention(qkvb, bias, n_heads=na_heads, n_lat=n_lat, n_ctx=n_ctx)
            xr = _matmul_w(gn, ab_w_out, i, x2=al, rows=rows, tn=512, epilogue="resid", res=xr, gate=mod[:, 2, :],
                           n_lat=n_lat, name="mixer_out")
        else:
            p = _matmul_w(xr, c_w_qkv, i, norm=(norm1_w[layer], scsh1), n_lat=n_lat, name="c_in_qkv")
            wvec = jnp.concatenate([jnp.tile(c_q_norm[i] * qk_scale, attn_heads), jnp.tile(c_k_norm[i], kv_heads),
                                    jnp.ones((kv_heads * HEAD_DIM,), F32)])
            qk, vt = _prep(p, wvec, attn_heads + kv_heads, attn_heads + kv_heads, cos, sin, n_vt=kv_heads)
            att = _attention(qk, vt, n_heads=attn_heads, n_kv=kv_heads, n_lat=n_lat, q_rows=rows)
            xr = _matmul_w(att, c_w_out, i, rows=rows, epilogue="resid", res=xr, gate=mod[:, 2, :], n_lat=n_lat,
                           name="mixer_out")
        a = _matmul_w(xr, w_mlp_in, layer, norm=(norm2_w[layer], scsh2), out_dtype=BF16, epilogue="relu2",
                      n_lat=n_lat, name="mlp_in")
        xr = _matmul(a, w_mlp_out_b, layer, epilogue="resid", res=xr, gate=mod[:, 5, :], n_lat=n_lat,
                     name="mlp_out")
    return xr[None]
```

```python
import functools
import math

import numpy as np
import jax
import jax.numpy as jnp
from jax import lax
from jax.experimental import pallas as pl
from jax.experimental.pallas import tpu as pltpu

F32 = jnp.float32
BF16 = jnp.bfloat16

GRID_W = 64
HEAD_DIM = 128
ROPE_THETA = 10000.0
RMS_EPS = 1e-6
N_MOD = 6
SSM_HEAD_DIM = 64
SSM_GROUPS = 8
SSM_STATE = 128
SSM_NORM_GROUPS = 8
CONV_K = 5
NA_ROWS = 8
NA_COLS = 16

LANES = 128
SUBLANES = 8
VMEM_LIMIT_BYTES = 48 * 1024 * 1024
VMEM_LIMIT_WEIGHT_RESIDENT = 56 * 1024 * 1024

ROW_TILE = 256
SSD_CHUNK = 256
VT_ROWS = HEAD_DIM + 2 * SUBLANES
NA_QROWS = 4
NA_WROWS = 12
NA_HEADS_PER_STEP = 4
ATTN_Q_TILE = 512


def _pick(n, candidates):
    for c in candidates:
        if n % c == 0:
            return c
    raise ValueError(f"no tile in {candidates} divides {n}")


def _cparams(sem):
    return pltpu.CompilerParams(dimension_semantics=sem, vmem_limit_bytes=VMEM_LIMIT_BYTES)


def _sigmoid(v):
    return 1.0 / (1.0 + jnp.exp(-v))


def _softplus(v):
    return jnp.maximum(v, 0.0) + jnp.log(1.0 + jnp.exp(-jnp.abs(v)))


def _dot(a, b):
    return jnp.dot(a, b, preferred_element_type=F32)


def _dot_nt(a, b):
    return lax.dot_general(a, b, (((1,), (1,)), ((), ())), preferred_element_type=F32)


def _split3(x):
    hi = x.astype(BF16)
    r1 = x - hi.astype(F32)
    mid = r1.astype(BF16)
    lo = (r1 - mid.astype(F32)).astype(BF16)
    return hi, mid, lo


def _mod_kernel(v_ref, w_ref, b_ref, o_ref):
    v = v_ref[...]
    s = (v * _sigmoid(v)).astype(BF16)
    o_ref[0] = _dot(s, w_ref[0].astype(BF16)) + b_ref[0]


def _modulation(vecs, w_mod, b_mod):
    depth, d, n = w_mod.shape
    tn = _pick(n, (1024, 512, 256, 128))
    return pl.pallas_call(
        _mod_kernel,
        grid=(depth, n // tn),
        in_specs=[pl.BlockSpec((SUBLANES, d), lambda l, j: (0, 0)),
                  pl.BlockSpec((1, d, tn), lambda l, j: (l, 0, j)),
                  pl.BlockSpec((1, 1, tn), lambda l, j: (l, 0, j))],
        out_specs=pl.BlockSpec((1, SUBLANES, tn), lambda l, j: (l, 0, j)),
        out_shape=jax.ShapeDtypeStruct((depth, SUBLANES, n), F32),
        compiler_params=_cparams(("arbitrary", "arbitrary")),
        name="modulation",
    )(vecs, w_mod, b_mod.reshape(depth, 1, n))


def _normmod_kernel(x_ref, w_ref, m_ref, o_ref):
    x = x_ref[...]
    y = x * lax.rsqrt(jnp.mean(x * x, axis=-1, keepdims=True) + RMS_EPS)
    m = m_ref[0]
    o_ref[...] = ((y * w_ref[...]) * (1.0 + m[0:1, :]) + m[1:2, :]).astype(o_ref.dtype)


def _normmod(x, w, scsh, n_lat):
    r, d = x.shape
    nlb = n_lat // ROW_TILE
    return pl.pallas_call(
        _normmod_kernel,
        grid=(r // ROW_TILE,),
        in_specs=[pl.BlockSpec((ROW_TILE, d), lambda i: (i, 0)),
                  pl.BlockSpec((1, d), lambda i: (0, 0)),
                  pl.BlockSpec((1, 2, d), lambda i: (i // nlb, 0, 0))],
        out_specs=pl.BlockSpec((ROW_TILE, d), lambda i: (i, 0)),
        out_shape=jax.ShapeDtypeStruct((r, d), BF16),
        compiler_params=_cparams(("arbitrary",)),
        name="normmod",
    )(x, w.reshape(1, d), scsh)


def _mm_kernel(*refs, nk, nk1, two_lhs, epilogue, n_lat, tm):
    refs = list(refs)
    x_ref = refs.pop(0)
    x2_ref = refs.pop(0) if two_lhs else None
    w_ref = refs.pop(0)
    if epilogue == "resid":
        res_ref = refs.pop(0)
        gate_ref = refs.pop(0)
    o_ref = refs.pop(0)

    def finish(acc):
        if epilogue == "relu2":
            a = jnp.maximum(acc, 0.0)
            o_ref[...] = (a * a).astype(o_ref.dtype)
        elif epilogue == "resid":
            row = pl.program_id(0) * tm + lax.broadcasted_iota(jnp.int32, (tm, 1), 0)
            g = jnp.where(row < n_lat, gate_ref[0:1, :], gate_ref[1:2, :])
            o_ref[...] = res_ref[...] + g * acc
        else:
            o_ref[...] = acc.astype(o_ref.dtype)

    if nk == 1:
        finish(_dot(x_ref[...], w_ref[0]))
        return
    acc_ref, = refs
    k = pl.program_id(2)

    @pl.when(k == 0)
    def _():
        acc_ref[...] = jnp.zeros_like(acc_ref)

    if two_lhs:
        @pl.when(k < nk1)
        def _():
            acc_ref[...] += _dot(x_ref[...], w_ref[0])

        @pl.when(k >= nk1)
        def _():
            acc_ref[...] += _dot(x2_ref[...], w_ref[0])
    else:
        acc_ref[...] += _dot(x_ref[...], w_ref[0])

    @pl.when(k == nk - 1)
    def _():
        finish(acc_ref[...])


def _matmul(x, w, layer, *, x2=None, rows=None, n_cols=None, out_dtype=F32, epilogue="none", res=None, gate=None,
            n_lat=0, name="matmul"):
    r = x.shape[0] if rows is None else rows
    _, kdim, n_w = w.shape
    n = n_w if n_cols is None else n_cols
    k1 = x.shape[1]
    tm = _pick(r, (1024, 768, 512, 256))
    tn = _pick(n, (1024, 512, 256, 128))
    if x2 is None:
        tk = kdim if kdim <= 2048 else _pick(kdim, (2048, 1024, 512))
    else:
        tk = math.gcd(k1, x2.shape[1])
    nk, nk1 = kdim // tk, k1 // tk
    in_specs = [pl.BlockSpec((tm, tk), lambda i, j, k: (i, jnp.minimum(k, nk1 - 1)))]
    args = [x]
    if x2 is not None:
        in_specs.append(pl.BlockSpec((tm, tk), lambda i, j, k: (i, jnp.maximum(k - nk1, 0))))
        args.append(x2)
    in_specs.append(pl.BlockSpec((1, tk, tn), lambda i, j, k: (layer, k, j)))
    args.append(w)
    if epilogue == "resid":
        in_specs += [pl.BlockSpec((tm, tn), lambda i, j, k: (i, j)),
                     pl.BlockSpec((2, tn), lambda i, j, k: (0, j))]
        args += [res, gate]
    kernel = functools.partial(_mm_kernel, nk=nk, nk1=nk1, two_lhs=x2 is not None, epilogue=epilogue, n_lat=n_lat,
                               tm=tm)
    return pl.pallas_call(
        kernel,
        grid=(r // tm, n // tn, nk),
        in_specs=in_specs,
        out_specs=pl.BlockSpec((tm, tn), lambda i, j, k: (i, j)),
        out_shape=jax.ShapeDtypeStruct((r, n), out_dtype),
        scratch_shapes=[] if nk == 1 else [pltpu.VMEM((tm, tn), F32)],
        compiler_params=_cparams(("arbitrary", "arbitrary", "arbitrary")),
        name=name,
    )(*args)


def _mmw_kernel(*refs, prologue, two_lhs, w_rows, epilogue, n_lat, tm, k1, qk_tiles):
    refs = list(refs)
    x_ref = refs.pop(0)
    if prologue:
        nw_ref = refs.pop(0)
        mod_ref = refs.pop(0)
    x2_ref = refs.pop(0) if two_lhs else None
    w_ref = refs.pop(0)
    if epilogue == "resid":
        res_ref = refs.pop(0)
        gate_ref = refs.pop(0)
    if epilogue == "qknorm":
        qkw_ref = refs.pop(0)
    o_ref, wb_ref = refs[:2]
    j = pl.program_id(0)
    i = pl.program_id(1)

    @pl.when(i == 0)
    def _():
        wb_ref[...] = w_ref[0].astype(BF16)

    if prologue:
        h_ref = refs[2]
        for s in range(tm // ROW_TILE):
            rs = slice(s * ROW_TILE, (s + 1) * ROW_TILE)
            ctx_rows = i * tm + s * ROW_TILE >= n_lat
            scale = nw_ref[...] * (1.0 + jnp.where(ctx_rows, mod_ref[1, 0:1, :], mod_ref[0, 0:1, :]))
            shift = jnp.where(ctx_rows, mod_ref[1, 1:2, :], mod_ref[0, 1:2, :])
            x = x_ref[rs, :]
            y = x * lax.rsqrt(jnp.mean(x * x, axis=-1, keepdims=True) + RMS_EPS)
            h_ref[rs, :] = (y * scale + shift).astype(BF16)
        h = h_ref[...]
    else:
        h = x_ref[...]
    if two_lhs:
        acc = _dot(h, wb_ref[0:k1, :]) + _dot(x2_ref[...], wb_ref[k1:, :])
    elif w_rows:
        acc = _dot_nt(h, wb_ref[...])
    else:
        acc = _dot(h, wb_ref[...])
    if epilogue == "relu2":
        a = jnp.maximum(acc, 0.0)
        o_ref[...] = (a * a).astype(o_ref.dtype)
    elif epilogue == "resid":
        is_lat = (i * tm + lax.broadcasted_iota(jnp.int32, (tm, 1), 0)) < n_lat
        o_ref[...] = res_ref[...] + jnp.where(is_lat, gate_ref[0:1, :], gate_ref[1:2, :]) * acc
    elif epilogue == "qknorm":
        @pl.when(j < qk_tiles)
        def _():
            for b in range(acc.shape[1] // HEAD_DIM):
                sl = slice(b * HEAD_DIM, (b + 1) * HEAD_DIM)
                u = acc[:, sl]
                u = u * lax.rsqrt(jnp.mean(u * u, axis=-1, keepdims=True) + RMS_EPS) * qkw_ref[:, sl]
                o_ref[:, sl] = u.astype(o_ref.dtype)

        @pl.when(j >= qk_tiles)
        def _():
            o_ref[...] = acc.astype(o_ref.dtype)
    else:
        o_ref[...] = acc.astype(o_ref.dtype)


def _matmul_w(x, w, layer, *, norm=None, x2=None, rows=None, w_rows=False, col0=0, n_cols=None, tn=1024, out_dtype=F32,
              epilogue="none", res=None, gate=None, qk_w=None, n_lat=0, name="matmul_w"):
    r = x.shape[0] if rows is None else rows
    kdim, n_w = (w.shape[2], w.shape[1]) if w_rows else (w.shape[1], w.shape[2])
    n = n_w - col0 if n_cols is None else n_cols
    tn = min(tn, n)
    k1 = x.shape[1]
    tm = _pick(r, (1024, 768, 512, 256))
    in_specs = [pl.BlockSpec((tm, k1), lambda j, i: (i, 0))]
    args = [x]
    scratch = [pltpu.VMEM((tn, kdim) if w_rows else (kdim, tn), BF16)]
    if norm is not None:
        norm_w, scsh = norm
        in_specs += [pl.BlockSpec((1, k1), lambda j, i: (0, 0)),
                     pl.BlockSpec((2, 2, k1), lambda j, i: (0, 0, 0))]
        args += [norm_w.reshape(1, k1), scsh]
        scratch.append(pltpu.VMEM((tm, k1), BF16))
    if x2 is not None:
        in_specs.append(pl.BlockSpec((tm, x2.shape[1]), lambda j, i: (i, 0)))
        args.append(x2)
    if w_rows:
        in_specs.append(pl.BlockSpec((pl.Element(1), pl.Element(tn), pl.Element(kdim)),
                                     lambda j, i: (layer, pl.multiple_of(col0 + j * tn, SUBLANES), 0)))
    else:
        assert col0 == 0
        in_specs.append(pl.BlockSpec((1, kdim, tn), lambda j, i: (layer, 0, j)))
    args.append(w)
    if epilogue == "resid":
        in_specs += [pl.BlockSpec((tm, tn), lambda j, i: (i, j)),
                     pl.BlockSpec((2, tn), lambda j, i: (0, j))]
        args += [res, gate]
    qk_tiles = 0
    if epilogue == "qknorm":
        qk_tiles = qk_w.shape[0] // tn
        assert qk_w.shape[0] % tn == 0
        in_specs.append(pl.BlockSpec((1, tn), lambda j, i: (0, jnp.minimum(j, qk_tiles - 1))))
        args.append(qk_w.reshape(1, -1))
    kernel = functools.partial(_mmw_kernel, prologue=norm is not None, two_lhs=x2 is not None, w_rows=w_rows,
                               epilogue=epilogue, n_lat=n_lat, tm=tm, k1=k1, qk_tiles=qk_tiles)
    return pl.pallas_call(
        kernel,
        grid=(n // tn, r // tm),
        in_specs=in_specs,
        out_specs=pl.BlockSpec((tm, tn), lambda j, i: (i, j)),
        out_shape=jax.ShapeDtypeStruct((r, n), out_dtype),
        scratch_shapes=scratch,
        compiler_params=pltpu.CompilerParams(dimension_semantics=("arbitrary", "arbitrary"),
                                             vmem_limit_bytes=VMEM_LIMIT_WEIGHT_RESIDENT),
        name=name,
    )(*args)


def _prep_kernel(*refs, n_norm, n_rope, n_blocks, n_vt):
    refs = list(refs)
    p_ref, w_ref = refs[:2]
    if n_rope:
        cos = refs[2][...]
        sin = refs[3][...]
        even = (lax.broadcasted_iota(jnp.int32, cos.shape, 1) % 2) == 0
    o_ref = refs[-2] if n_vt else refs[-1]
    for b in range(n_blocks):
        sl = slice(b * HEAD_DIM, (b + 1) * HEAD_DIM)
        u = p_ref[:, sl]
        if b < n_norm:
            u = u * lax.rsqrt(jnp.mean(u * u, axis=-1, keepdims=True) + RMS_EPS) * w_ref[:, sl]
        if b < n_rope:
            partner = jnp.where(even, pltpu.roll(u, HEAD_DIM - 1, 1), pltpu.roll(u, 1, 1))
            u = u * cos + partner * sin
        if b < n_blocks - n_vt:
            o_ref[:, sl] = u.astype(o_ref.dtype)
        else:
            vt_ref = refs[-1]
            hv = b - (n_blocks - n_vt)
            vt_ref[hv, 0, 0:HEAD_DIM, :] = u.T.astype(BF16)
            pad_row = lax.broadcasted_iota(jnp.int32, (VT_ROWS - HEAD_DIM, u.shape[0]), 0)
            vt_ref[hv, 0, HEAD_DIM:, :] = jnp.where(pad_row == 0, 1.0, 0.0).astype(BF16)


def _attn_key_tile(n_all):
    for tk in (768, ROW_TILE):
        if n_all % tk == 0 and (n_all // tk) % 2 == 1:
            return tk
    raise ValueError(f"no key tile for {n_all} rows")


def _prep(p, wvec, n_norm, n_rope=0, cos=None, sin=None, n_vt=0):
    r, n = p.shape
    n_blocks = n // HEAD_DIM
    n_out = (n_blocks - n_vt) * HEAD_DIM
    tk = _attn_key_tile(r) if n_vt else ROW_TILE
    per = tk // ROW_TILE
    in_specs = [pl.BlockSpec((ROW_TILE, n), lambda i: (i, 0)),
                pl.BlockSpec((1, n), lambda i: (0, 0))]
    args = [p, wvec.reshape(1, n)]
    if n_rope:
        in_specs += [pl.BlockSpec((ROW_TILE, HEAD_DIM), lambda i: (i, 0))] * 2
        args += [cos, sin]
    out_specs = [pl.BlockSpec((ROW_TILE, n_out), lambda i: (i, 0))]
    out_shape = [jax.ShapeDtypeStruct((r, n_out), BF16)]
    if n_vt:
        out_specs.append(pl.BlockSpec((n_vt, 1, VT_ROWS, ROW_TILE), lambda i: (0, i // per, 0, i % per)))
        out_shape.append(jax.ShapeDtypeStruct((n_vt, r // tk, VT_ROWS, tk), BF16))
    out = pl.pallas_call(
        functools.partial(_prep_kernel, n_norm=n_norm, n_rope=n_rope, n_blocks=n_blocks, n_vt=n_vt),
        grid=(r // ROW_TILE,),
        in_specs=in_specs,
        out_specs=out_specs,
        out_shape=out_shape,
        compiler_params=_cparams(("arbitrary",)),
        name="qkv_prep",
    )(*args)
    return out if n_vt else out[0]


def _attn_kernel(q_ref, k_ref, vt_ref, o_ref, s0_ref, s1_ref, m_ref, acc_ref, *, group, tq, tk, n_lat, n_all):
    n_tiles = n_all // tk
    n_ctx = n_all - n_lat
    m_ref[...] = jnp.full(m_ref.shape, -jnp.inf, F32)
    acc_ref[...] = jnp.zeros(acc_ref.shape, F32)

    def scores(kc, s_ref, g):
        st = _dot_nt(kc, q_ref[:, g * HEAD_DIM:(g + 1) * HEAD_DIM])
        s_ref[g, 0:kc.shape[0], :] = st
        s_ref[g, tk:tk + 1, :] = jnp.max(st, axis=0, keepdims=True)

    def consume(vtc, s_ref, g):
        st = s_ref[g, 0:vtc.shape[1], :]
        m_prev = m_ref[g]
        m_new = jnp.maximum(m_prev, s_ref[g, tk:tk + 1, :])
        p = jnp.exp2(st - m_new).astype(BF16)
        acc_ref[g] = jnp.exp2(m_prev - m_new) * acc_ref[g] + _dot(vtc, p)
        m_ref[g] = m_new

    def key_tile(c):
        return k_ref[pl.ds(pl.multiple_of(c * tk, tk), tk), :]

    def step(k_next, vt_cur, s_next, s_cur):
        for g in range(group):
            if k_next is not None:
                scores(k_next, s_next, g)
            if vt_cur is not None:
                consume(vt_cur, s_cur, g)

    def pair(i, carry):
        c = 2 * i
        step(key_tile(c + 1), vt_ref[0, c], s1_ref, s0_ref)
        step(key_tile(c + 2), vt_ref[0, c + 1], s0_ref, s1_ref)
        return carry

    @pl.when(pl.program_id(1) < n_lat // tq)
    def _():
        step(key_tile(0), None, s0_ref, None)
        lax.fori_loop(0, (n_tiles - 1) // 2, pair, 0)
        step(None, vt_ref[0, n_tiles - 1], None, s0_ref)

    @pl.when(pl.program_id(1) >= n_lat // tq)
    def _():
        step(k_ref[n_lat:n_all, :], None, s0_ref, None)
        step(None, vt_ref[0, n_tiles - 1, :, tk - n_ctx:tk], None, s0_ref)

    for g in range(group):
        out = acc_ref[g, 0:HEAD_DIM, :] / acc_ref[g, HEAD_DIM:HEAD_DIM + 1, :]
        o_ref[:, g * HEAD_DIM:(g + 1) * HEAD_DIM] = out.T.astype(o_ref.dtype)


def _attention(qk, vt, *, n_heads, n_kv, n_lat, q_rows):
    n_all = qk.shape[0]
    group = n_heads // n_kv
    tq = ATTN_Q_TILE
    tk = vt.shape[-1]
    gw = group * HEAD_DIM
    assert n_all - n_lat <= min(tk, tq) and n_lat % tq == 0
    kernel = functools.partial(_attn_kernel, group=group, tq=tq, tk=tk, n_lat=n_lat, n_all=n_all)
    return pl.pallas_call(
        kernel,
        grid=(n_kv, pl.cdiv(q_rows, tq)),
        in_specs=[pl.BlockSpec((tq, gw), lambda h, i: (i, h)),
                  pl.BlockSpec((n_all, HEAD_DIM), lambda h, i: (0, n_heads + h)),
                  pl.BlockSpec((1, n_all // tk, VT_ROWS, tk), lambda h, i: (h, 0, 0, 0))],
        out_specs=pl.BlockSpec((tq, gw), lambda h, i: (i, h)),
        out_shape=jax.ShapeDtypeStruct((q_rows, n_heads * HEAD_DIM), BF16),
        scratch_shapes=[pltpu.VMEM((group, tk + SUBLANES, tq), F32),
                        pltpu.VMEM((group, tk + SUBLANES, tq), F32),
                        pltpu.VMEM((group, 1, tq), F32),
                        pltpu.VMEM((group, VT_ROWS, tq), F32)],
        compiler_params=_cparams(("arbitrary", "arbitrary")),
        name="gqa_attention",
    )(qk, qk, vt)


def _na_kernel(q_ref, k_ref, v_ref, bias_ref, o_ref, *, n_lat, rows, n_ctx):
    nb = rows // NA_QROWS
    b = pl.program_id(1)
    heads = [slice(j * HEAD_DIM, (j + 1) * HEAD_DIM) for j in range(NA_HEADS_PER_STEP)]
    s_c = [_dot_nt(q_ref[:, hs], k_ref[n_lat:n_lat + n_ctx, hs]) for hs in heads]

    @pl.when(b < nb)
    def _():
        w0 = jnp.clip(NA_QROWS * b - NA_ROWS // 2, 0, rows - NA_WROWS)
        win = pl.ds(pl.multiple_of(w0 * GRID_W, GRID_W), NA_WROWS * GRID_W)
        s_w = [_dot_nt(q_ref[:, hs], k_ref[win, hs]) + bias_ref[0, j] for j, hs in enumerate(heads)]
        m = [jnp.maximum(jnp.max(sw, axis=1, keepdims=True), jnp.max(sc, axis=1, keepdims=True))
             for sw, sc in zip(s_w, s_c)]
        p_w = [jnp.exp2(sw - mj) for sw, mj in zip(s_w, m)]
        p_c = [jnp.exp2(sc - mj) for sc, mj in zip(s_c, m)]
        for j, hs in enumerate(heads):
            l = jnp.sum(p_w[j], axis=1, keepdims=True) + jnp.sum(p_c[j], axis=1, keepdims=True)
            o = _dot(p_w[j].astype(BF16), v_ref[win, hs]) + _dot(p_c[j].astype(BF16), v_ref[n_lat:n_lat + n_ctx, hs])
            o_ref[:, hs] = (o / l).astype(o_ref.dtype)

    @pl.when(b >= nb)
    def _():
        for j, hs in enumerate(heads):
            p_c = jnp.exp2(s_c[j] - jnp.max(s_c[j], axis=1, keepdims=True))
            l = jnp.sum(p_c, axis=1, keepdims=True)
            o_ref[:, hs] = (_dot(p_c.astype(BF16), v_ref[n_lat:n_lat + n_ctx, hs]) / l).astype(o_ref.dtype)


def _na_bias_tiles(rpb, rows):
    h = rpb.shape[0]
    col = np.arange(GRID_W)
    c0 = np.clip(col - NA_COLS // 2, 0, GRID_W - NA_COLS)
    col_ok = (col[None, :] >= c0[:, None]) & (col[None, :] < c0[:, None] + NA_COLS)
    pad = GRID_W - NA_COLS
    rp = jnp.pad(rpb * math.log2(math.e), ((0, 0), (0, 0), (pad, pad)))
    toep = jnp.stack([rp[:, :, NA_COLS - 1 - q + pad:NA_COLS - 1 - q + pad + GRID_W] for q in range(GRID_W)], axis=2)
    toep = jnp.where(col_ok[None, None], toep, -jnp.inf)
    neg = jnp.full((h, GRID_W, GRID_W), -jnp.inf, F32)
    half = NA_ROWS // 2
    kinds = (lambda j: (j, 0), lambda j: (half + j, j), lambda j: (NA_ROWS + j, half))
    tiles = []
    for kind in kinds:
        qrows = []
        for j in range(NA_QROWS):
            q_row, win = kind(j)
            blocks = []
            for ik in range(NA_WROWS):
                dr = ik - q_row + (NA_ROWS - 1)
                ok = win <= ik < win + NA_ROWS
                blocks.append(toep[:, dr] if ok else neg)
            qrows.append(jnp.concatenate(blocks, axis=-1))
        tiles.append(jnp.concatenate(qrows, axis=-2))
    return jnp.stack(tiles, axis=0)


def _neighbourhood_attention(qkv, bias, *, n_heads, n_lat, n_ctx):
    r = qkv.shape[0]
    rows = n_lat // GRID_W
    nb = rows // NA_QROWS
    tq = NA_QROWS * GRID_W
    assert tq == n_ctx and rows >= NA_WROWS + NA_QROWS

    def bias_map(h, b):
        return (jnp.where(b == 0, 0, jnp.where(b >= nb - 1, 2, 1)), h, 0, 0)

    kernel = functools.partial(_na_kernel, n_lat=n_lat, rows=rows, n_ctx=n_ctx)
    hps = NA_HEADS_PER_STEP
    hw = hps * HEAD_DIM
    n_steps = n_heads // hps
    return pl.pallas_call(
        kernel,
        grid=(n_steps, r // tq),
        in_specs=[pl.BlockSpec((tq, hw), lambda h, b: (b, h)),
                  pl.BlockSpec((r, hw), lambda h, b: (0, n_steps + h)),
                  pl.BlockSpec((r, hw), lambda h, b: (0, 2 * n_steps + h)),
                  pl.BlockSpec((1, hps, tq, NA_WROWS * GRID_W), bias_map)],
        out_specs=pl.BlockSpec((tq, hw), lambda h, b: (b, h)),
        out_shape=jax.ShapeDtypeStruct((r, n_heads * HEAD_DIM), BF16),
        compiler_params=_cparams(("arbitrary", "arbitrary")),
        name="neighbourhood_attention",
    )(qkv, qkv, qkv, bias)


def _conv_kernel(u_ref, prev_ref, next_ref, w_ref, b_ref, o_ref, ext_ref, *, n_lat):
    tr = u_ref.shape[0]
    i = pl.program_id(0)
    nlb = n_lat // tr
    has_prev = jnp.logical_and(i != 0, i != nlb)
    has_next = i < nlb - 1
    ext_ref[0:SUBLANES, :] = jnp.where(has_prev, prev_ref[...], 0.0)
    ext_ref[SUBLANES:SUBLANES + tr, :] = u_ref[...]
    ext_ref[SUBLANES + tr:, :] = jnp.where(has_next, next_ref[...], 0.0)
    ext = ext_ref[...]
    n_ext = ext.shape[0]
    acc = jnp.broadcast_to(b_ref[...], o_ref.shape)
    for j in range(CONV_K):
        d = j - CONV_K // 2
        shifted = ext if d == 0 else pltpu.roll(ext, (n_ext - d) % n_ext, 0)
        acc = acc + w_ref[j:j + 1, :] * shifted[SUBLANES:SUBLANES + tr, :]
    o_ref[...] = (acc * _sigmoid(acc)).astype(o_ref.dtype)


def _conv_silu(proj, col0, conv_w, conv_b, ch0, n_ch, n_lat, out_dtype):
    r = proj.shape[0]
    ch = conv_w.shape[1]
    tr = ROW_TILE
    tc = _pick(n_ch, (1024, 512, 256, 128))
    assert (col0 + ch0) % tc == 0 and ch0 % tc == 0
    cb0 = (col0 + ch0) // tc
    wb0 = ch0 // tc
    rb = tr // SUBLANES
    last = r // SUBLANES - 1
    return pl.pallas_call(
        functools.partial(_conv_kernel, n_lat=n_lat),
        grid=(r // tr, n_ch // tc),
        in_specs=[pl.BlockSpec((tr, tc), lambda i, j: (i, cb0 + j)),
                  pl.BlockSpec((SUBLANES, tc), lambda i, j: (jnp.maximum(i * rb - 1, 0), cb0 + j)),
                  pl.BlockSpec((SUBLANES, tc), lambda i, j: (jnp.minimum((i + 1) * rb, last), cb0 + j)),
                  pl.BlockSpec((CONV_K, tc), lambda i, j: (0, wb0 + j)),
                  pl.BlockSpec((1, tc), lambda i, j: (0, wb0 + j))],
        out_specs=pl.BlockSpec((tr, tc), lambda i, j: (i, j)),
        out_shape=jax.ShapeDtypeStruct((r, n_ch), out_dtype),
        scratch_shapes=[pltpu.VMEM((tr + 2 * SUBLANES, tc), F32)],
        compiler_params=_cparams(("arbitrary", "arbitrary")),
        name="conv_silu",
    )(proj, proj, proj, conv_w, conv_b.reshape(1, ch))


def _ssd_kernel(xs_ref, b_ref, c_ref, dt_ref, dtt_ref, bias_ref, biast_ref, alog_ref, alogt_ref, e_ref,
                y_ref, st_ref, *, n_groups, heads_per_group):
    t = xs_ref.shape[0]
    d = pl.program_id(0)
    c = pl.program_id(1)
    gw = heads_per_group * SSM_HEAD_DIM

    @pl.when(c == 0)
    def _():
        st_ref[...] = jnp.zeros(st_ref.shape, F32)

    dt = _softplus(dt_ref[0] + bias_ref[0])
    dtt = _softplus(dtt_ref[0] + biast_ref[0])
    da = dt * (-jnp.exp(alog_ref[0]))
    dat = dtt * (-jnp.exp(alogt_ref[0]))
    row = lax.broadcasted_iota(jnp.int32, (t, t), 0)
    col = lax.broadcasted_iota(jnp.int32, (t, t), 1)
    lag = jnp.where(d == 0, row - col, col - row)
    valid = lag >= 0
    ones_v = jnp.where(valid, 1.0, 0.0).astype(BF16)
    ones_vt = jnp.where(lag <= 0, 1.0, 0.0).astype(BF16)
    cum = sum(_dot(ones_v, piece) for piece in _split3(da))
    cumt = sum(_dot(piece, ones_vt) for piece in _split3(dat))
    tot = jnp.sum(da, axis=0, keepdims=True)
    log2e = math.log2(math.e)
    cum_l2 = cum * log2e
    cumt_s2 = (cumt - jnp.log(dtt)) * log2e
    unreached = jnp.where(valid, 0.0, -jnp.inf)
    stack = jnp.concatenate([dt * jnp.exp(tot - cum), jnp.exp(cum),
                             jnp.broadcast_to(jnp.exp(tot), (SUBLANES, tot.shape[1]))], axis=0)
    wide = _dot(jnp.concatenate(_split3(stack), axis=1), e_ref[...])
    win_w = wide[0:t]
    ecum_w = wide[t:2 * t]
    cdec_w = wide[2 * t:2 * t + 1]
    xs = xs_ref[...]
    xin = (xs * win_w).astype(BF16)
    lane_head = lax.broadcasted_iota(jnp.int32, (t, gw), 1) // SSM_HEAD_DIM

    for g in range(n_groups):
        ns = slice(g * SSM_STATE, (g + 1) * SSM_STATE)
        hs = slice(g * gw, (g + 1) * gw)
        bg = b_ref[:, ns]
        cg = c_ref[:, ns]
        cb = _dot_nt(cg, bg)
        acc = _dot(cg, st_ref[g].astype(BF16)) * ecum_w[:, hs]
        xs_g = xs[:, hs].astype(BF16)
        for k in range(heads_per_group):
            h = g * heads_per_group + k
            decay = jnp.exp2((cum_l2[:, h:h + 1] - cumt_s2[h:h + 1, :]) + unreached)
            mh = (cb * decay).astype(BF16)
            xk = jnp.where(lane_head == k, xs_g, jnp.zeros_like(xs_g))
            acc = acc + _dot(mh, xk)
        y_ref[0, :, hs] = acc
        st_ref[g] = st_ref[g] * cdec_w[:, hs] + _dot(bg.astype(F32).T.astype(BF16), xin[:, hs])


def _ssd(xs, bc, dt_raw, dt_bias, a_log, n_lat):
    r, n_inner = xs.shape
    t = SSD_CHUNK
    heads = n_inner // SSM_HEAD_DIM
    gn = SSM_GROUPS * SSM_STATE
    hpg = heads // SSM_GROUPS
    n_lc = n_lat // t
    n_cc = (r - n_lat) // t
    assert (r - n_lat) % t == 0 and n_inner % gn == 0
    dt3 = dt_raw[:, :2 * heads].reshape(r, 2, heads).transpose(1, 0, 2)
    dt3t = dt3.transpose(0, 2, 1)
    expand = jnp.asarray(np.tile(np.repeat(np.eye(heads, dtype=np.float32), SSM_HEAD_DIM, axis=1), (3, 1)), BF16)

    def chunk(d, c):
        fwd = jnp.where(c < n_cc, n_lc + c, c - n_cc)
        bwd = n_lc + n_cc - 1 - c
        return jnp.where(d == 0, fwd, bwd)

    kernel = functools.partial(_ssd_kernel, n_groups=SSM_GROUPS, heads_per_group=hpg)
    return pl.pallas_call(
        kernel,
        grid=(2, n_lc + n_cc),
        in_specs=[pl.BlockSpec((t, n_inner), lambda d, c: (chunk(d, c), 0)),
                  pl.BlockSpec((t, gn), lambda d, c: (chunk(d, c), 0)),
                  pl.BlockSpec((t, gn), lambda d, c: (chunk(d, c), 1)),
                  pl.BlockSpec((1, t, heads), lambda d, c: (d, chunk(d, c), 0)),
                  pl.BlockSpec((1, heads, t), lambda d, c: (d, 0, chunk(d, c))),
                  pl.BlockSpec((1, 1, heads), lambda d, c: (d, 0, 0)),
                  pl.BlockSpec((1, heads, 1), lambda d, c: (d, 0, 0)),
                  pl.BlockSpec((1, 1, heads), lambda d, c: (d, 0, 0)),
                  pl.BlockSpec((1, heads, 1), lambda d, c: (d, 0, 0)),
                  pl.BlockSpec((3 * heads, n_inner), lambda d, c: (0, 0))],
        out_specs=pl.BlockSpec((1, t, n_inner), lambda d, c: (d, chunk(d, c), 0)),
        out_shape=jax.ShapeDtypeStruct((2, r, n_inner), F32),
        scratch_shapes=[pltpu.VMEM((SSM_GROUPS, SSM_STATE, hpg * SSM_HEAD_DIM), F32)],
        compiler_params=_cparams(("arbitrary", "arbitrary")),
        name="ssd_scan",
    )(xs, bc, bc, dt3, dt3t, dt_bias.reshape(2, 1, heads), dt_bias.reshape(2, heads, 1),
      a_log.reshape(2, 1, heads), a_log.reshape(2, heads, 1), expand)


def _gated_norm_kernel(y_ref, xs_ref, z_ref, dskip_ref, w_ref, o_ref, *, n_groups):
    z = z_ref[...]
    u = (y_ref[0] + y_ref[1] + dskip_ref[...] * xs_ref[...]) * (z * _sigmoid(z))
    gw = u.shape[1] // n_groups
    for g in range(n_groups):
        sl = slice(g * gw, (g + 1) * gw)
        ug = u[:, sl]
        ug = ug * lax.rsqrt(jnp.mean(ug * ug, axis=-1, keepdims=True) + RMS_EPS)
        o_ref[:, sl] = (ug * w_ref[:, sl]).astype(o_ref.dtype)


def _gated_norm(y, xs, proj, dskip_w, norm_w):
    _, r, n = y.shape
    return pl.pallas_call(
        functools.partial(_gated_norm_kernel, n_groups=SSM_NORM_GROUPS),
        grid=(r // ROW_TILE,),
        in_specs=[pl.BlockSpec((2, ROW_TILE, n), lambda i: (0, i, 0)),
                  pl.BlockSpec((ROW_TILE, n), lambda i: (i, 0)),
                  pl.BlockSpec((ROW_TILE, n), lambda i: (i, 0)),
                  pl.BlockSpec((1, n), lambda i: (0, 0)),
                  pl.BlockSpec((1, n), lambda i: (0, 0))],
        out_specs=pl.BlockSpec((ROW_TILE, n), lambda i: (i, 0)),
        out_shape=jax.ShapeDtypeStruct((r, n), BF16),
        compiler_params=_cparams(("arbitrary",)),
        name="gated_norm",
    )(y, xs, proj, dskip_w.reshape(1, n), norm_w.reshape(1, n))


def _rope_tables(n_lat, n_ctx):
    t = jnp.arange(n_lat)
    half = HEAD_DIM // 2
    inv = 1.0 / (ROPE_THETA ** (jnp.arange(0, half, 2, dtype=F32) / half))
    ang = jnp.concatenate([(t // GRID_W).astype(F32)[:, None] * inv,
                           (t % GRID_W).astype(F32)[:, None] * inv], axis=-1)
    cos = jnp.repeat(jnp.cos(ang), 2, axis=-1)
    sin = jnp.repeat(jnp.sin(ang), 2, axis=-1) * jnp.asarray(np.tile(np.array([-1.0, 1.0], np.float32), half))
    cos = jnp.concatenate([cos, jnp.ones((n_ctx, HEAD_DIM), F32)], axis=0)
    sin = jnp.concatenate([sin, jnp.zeros((n_ctx, HEAD_DIM), F32)], axis=0)
    return cos, sin


def kernel(x, c, ctx, c_ctx, w_mod, b_mod, norm1_w, norm2_w, w_mlp_in, w_mlp_out, ab_w_in, ab_conv_w, ab_conv_b,
           ab_dt_bias, ab_a_log, ab_d_skip, ab_norm_w, ab_q_norm, ab_k_norm, ab_rpb, ab_w_out, c_w_qkv, c_q_norm,
           c_k_norm, c_w_out):
    _, n_lat, d = x.shape
    n_ctx = ctx.shape[1]
    depth = w_mod.shape[0]
    assert x.shape[0] == 1 and n_ctx == ROW_TILE and n_lat % 1024 == 0
    n_inner = d
    ssm_heads = n_inner // SSM_HEAD_DIM
    conv_dim = ab_conv_w.shape[-1]
    na_heads = ab_rpb.shape[1]
    attn_heads = c_w_out.shape[1] // HEAD_DIM
    kv_heads = (c_w_qkv.shape[-1] // HEAD_DIM - attn_heads) // 2
    qk_scale = HEAD_DIM ** -0.5 * math.log2(math.e)
    i0 = n_inner
    i1 = i0 + conv_dim
    i2 = i1 + 2 * ssm_heads

    w_mlp_out_b = w_mlp_out.astype(BF16)
    ab_w_in_t = jnp.swapaxes(ab_w_in, 1, 2)

    xr = jnp.concatenate([x[0], ctx[0]], axis=0)
    vecs = jnp.zeros((SUBLANES, d), F32).at[0].set(c[0]).at[1].set(c_ctx)
    mods = _modulation(vecs, w_mod, b_mod)[:, :2].reshape(depth, 2, N_MOD, d)
    cos, sin = _rope_tables(n_lat, n_ctx)

    for layer in range(depth):
        last = layer == depth - 1
        i = layer // 2
        mod = mods[layer]
        scsh1 = jnp.stack([mod[:, 1], mod[:, 0]], axis=1)
        scsh2 = jnp.stack([mod[:, 4], mod[:, 3]], axis=1)
        rows = n_lat if last else n_lat + n_ctx
        if layer % 2 == 0:
            h = _normmod(xr, norm1_w[layer], scsh1, n_lat)
            proj = _matmul_w(h, ab_w_in_t, i, w_rows=True, n_cols=i1, name="ab_in_zxbc")
            dt_raw = _matmul_w(h, ab_w_in_t, i, w_rows=True, col0=i1, n_cols=i2 - i1, name="ab_in_dt")
            qk_w = jnp.concatenate([jnp.tile(ab_q_norm[i] * qk_scale, na_heads), jnp.tile(ab_k_norm[i], na_heads)])
            qkvb = _matmul_w(h, ab_w_in_t, i, w_rows=True, col0=i2, out_dtype=BF16, epilogue="qknorm", qk_w=qk_w,
                             name="ab_in_qkv")
            xs = _conv_silu(proj, i0, ab_conv_w[i], ab_conv_b[i], 0, n_inner, n_lat, F32)
            bc = _conv_silu(proj, i0, ab_conv_w[i], ab_conv_b[i], n_inner, conv_dim - n_inner, n_lat, BF16)
            y = _ssd(xs, bc, dt_raw, ab_dt_bias[i], ab_a_log[i], n_lat)
            dskip = jnp.repeat(ab_d_skip[i, 0] + ab_d_skip[i, 1], SSM_HEAD_DIM)
            gn = _gated_norm(y, xs, proj, dskip, ab_norm_w[i])
            bias = _na_bias_tiles(ab_rpb[i], n_lat // GRID_W)
            al = _neighbourhood_attention(qkvb, bias, n_heads=na_heads, n_lat=n_lat, n_ctx=n_ctx)
            xr = _matmul_w(gn, ab_w_out, i, x2=al, rows=rows, tn=512, epilogue="resid", res=xr, gate=mod[:, 2, :],
                           n_lat=n_lat, name="mixer_out")
        else:
            p = _matmul_w(xr, c_w_qkv, i, norm=(norm1_w[layer], scsh1), n_lat=n_lat, name="c_in_qkv")
            wvec = jnp.concatenate([jnp.tile(c_q_norm[i] * qk_scale, attn_heads), jnp.tile(c_k_norm[i], kv_heads),
                                    jnp.ones((kv_heads * HEAD_DIM,), F32)])
            qk, vt = _prep(p, wvec, attn_heads + kv_heads, attn_heads + kv_heads, cos, sin, n_vt=kv_heads)
            att = _attention(qk, vt, n_heads=attn_heads, n_kv=kv_heads, n_lat=n_lat, q_rows=rows)
            xr = _matmul_w(att, c_w_out, i, rows=rows, epilogue="resid", res=xr, gate=mod[:, 2, :], n_lat=n_lat,
                           name="mixer_out")
        a = _matmul_w(xr, w_mlp_in, layer, norm=(norm2_w[layer], scsh2), out_dtype=BF16, epilogue="relu2",
                      n_lat=n_lat, name="mlp_in")
        xr = _matmul(a, w_mlp_out_b, layer, epilogue="resid", res=xr, gate=mod[:, 5, :], n_lat=n_lat,
                     name="mlp_out")
    return xr[None]
```

```python
import functools
import math

import numpy as np
import jax
import jax.numpy as jnp
from jax import lax
from jax.experimental import pallas as pl
from jax.experimental.pallas import tpu as pltpu

F32 = jnp.float32
BF16 = jnp.bfloat16

GRID_W = 64
HEAD_DIM = 128
ROPE_THETA = 10000.0
RMS_EPS = 1e-6
N_MOD = 6
SSM_HEAD_DIM = 64
SSM_GROUPS = 8
SSM_STATE = 128
SSM_NORM_GROUPS = 8
CONV_K = 5
NA_ROWS = 8
NA_COLS = 16

LANES = 128
SUBLANES = 8
VMEM_LIMIT_BYTES = 48 * 1024 * 1024
VMEM_LIMIT_WEIGHT_RESIDENT = 56 * 1024 * 1024

ROW_TILE = 256
SSD_CHUNK = 256
VT_ROWS = HEAD_DIM + 2 * SUBLANES
NA_QROWS = 4
NA_WROWS = 12
NA_HEADS_PER_STEP = 4
ATTN_Q_TILE = 512


def _pick(n, candidates):
    for c in candidates:
        if n % c == 0:
            return c
    raise ValueError(f"no tile in {candidates} divides {n}")


def _cparams(sem):
    return pltpu.CompilerParams(dimension_semantics=sem, vmem_limit_bytes=VMEM_LIMIT_BYTES)


def _sigmoid(v):
    return 1.0 / (1.0 + jnp.exp(-v))


def _softplus(v):
    return jnp.maximum(v, 0.0) + jnp.log(1.0 + jnp.exp(-jnp.abs(v)))


def _dot(a, b):
    return jnp.dot(a, b, preferred_element_type=F32)


def _dot_nt(a, b):
    return lax.dot_general(a, b, (((1,), (1,)), ((), ())), preferred_element_type=F32)


def _split3(x):
    hi = x.astype(BF16)
    r1 = x - hi.astype(F32)
    mid = r1.astype(BF16)
    lo = (r1 - mid.astype(F32)).astype(BF16)
    return hi, mid, lo


def _mod_kernel(v_ref, w_ref, b_ref, o_ref):
    v = v_ref[...]
    s = (v * _sigmoid(v)).astype(BF16)
    o_ref[0] = _dot(s, w_ref[0].astype(BF16)) + b_ref[0]


def _modulation(vecs, w_mod, b_mod):
    depth, d, n = w_mod.shape
    tn = _pick(n, (1024, 512, 256, 128))
    return pl.pallas_call(
        _mod_kernel,
        grid=(depth, n // tn),
        in_specs=[pl.BlockSpec((SUBLANES, d), lambda l, j: (0, 0)),
                  pl.BlockSpec((1, d, tn), lambda l, j: (l, 0, j)),
                  pl.BlockSpec((1, 1, tn), lambda l, j: (l, 0, j))],
        out_specs=pl.BlockSpec((1, SUBLANES, tn), lambda l, j: (l, 0, j)),
        out_shape=jax.ShapeDtypeStruct((depth, SUBLANES, n), F32),
        compiler_params=_cparams(("arbitrary", "arbitrary")),
        name="modulation",
    )(vecs, w_mod, b_mod.reshape(depth, 1, n))


def _normmod_kernel(x_ref, w_ref, m_ref, o_ref, *, n_lat, tm):
    for s in range(tm // ROW_TILE):
        rs = slice(s * ROW_TILE, (s + 1) * ROW_TILE)
        ctx_rows = pl.program_id(0) * tm + s * ROW_TILE >= n_lat
        scale = w_ref[...] * (1.0 + jnp.where(ctx_rows, m_ref[1, 0:1, :], m_ref[0, 0:1, :]))
        shift = jnp.where(ctx_rows, m_ref[1, 1:2, :], m_ref[0, 1:2, :])
        x = x_ref[rs, :]
        y = x * lax.rsqrt(jnp.mean(x * x, axis=-1, keepdims=True) + RMS_EPS)
        o_ref[rs, :] = (y * scale + shift).astype(o_ref.dtype)


def _normmod(x, w, scsh, n_lat):
    r, d = x.shape
    tm = _pick(r, (768, 512, 256))
    return pl.pallas_call(
        functools.partial(_normmod_kernel, n_lat=n_lat, tm=tm),
        grid=(r // tm,),
        in_specs=[pl.BlockSpec((tm, d), lambda i: (i, 0)),
                  pl.BlockSpec((1, d), lambda i: (0, 0)),
                  pl.BlockSpec((2, 2, d), lambda i: (0, 0, 0))],
        out_specs=pl.BlockSpec((tm, d), lambda i: (i, 0)),
        out_shape=jax.ShapeDtypeStruct((r, d), BF16),
        compiler_params=_cparams(("arbitrary",)),
        name="normmod",
    )(x, w.reshape(1, d), scsh)


def _mm_kernel(*refs, nk, nk1, two_lhs, epilogue, n_lat, tm):
    refs = list(refs)
    x_ref = refs.pop(0)
    x2_ref = refs.pop(0) if two_lhs else None
    w_ref = refs.pop(0)
    if epilogue == "resid":
        res_ref = refs.pop(0)
        gate_ref = refs.pop(0)
    o_ref = refs.pop(0)

    def finish(acc):
        if epilogue == "relu2":
            a = jnp.maximum(acc, 0.0)
            o_ref[...] = (a * a).astype(o_ref.dtype)
        elif epilogue == "resid":
            row = pl.program_id(0) * tm + lax.broadcasted_iota(jnp.int32, (tm, 1), 0)
            g = jnp.where(row < n_lat, gate_ref[0:1, :], gate_ref[1:2, :])
            o_ref[...] = res_ref[...] + g * acc
        else:
            o_ref[...] = acc.astype(o_ref.dtype)

    if nk == 1:
        finish(_dot(x_ref[...], w_ref[0]))
        return
    acc_ref, = refs
    k = pl.program_id(2)

    @pl.when(k == 0)
    def _():
        acc_ref[...] = jnp.zeros_like(acc_ref)

    if two_lhs:
        @pl.when(k < nk1)
        def _():
            acc_ref[...] += _dot(x_ref[...], w_ref[0])

        @pl.when(k >= nk1)
        def _():
            acc_ref[...] += _dot(x2_ref[...], w_ref[0])
    else:
        acc_ref[...] += _dot(x_ref[...], w_ref[0])

    @pl.when(k == nk - 1)
    def _():
        finish(acc_ref[...])


def _matmul(x, w, layer, *, x2=None, rows=None, n_cols=None, out_dtype=F32, epilogue="none", res=None, gate=None,
            n_lat=0, name="matmul"):
    r = x.shape[0] if rows is None else rows
    _, kdim, n_w = w.shape
    n = n_w if n_cols is None else n_cols
    k1 = x.shape[1]
    tm = _pick(r, (1024, 768, 512, 256))
    tn = _pick(n, (1024, 512, 256, 128))
    if x2 is None:
        tk = kdim if kdim <= 2048 else _pick(kdim, (2048, 1024, 512))
    else:
        tk = math.gcd(k1, x2.shape[1])
    nk, nk1 = kdim // tk, k1 // tk
    in_specs = [pl.BlockSpec((tm, tk), lambda i, j, k: (i, jnp.minimum(k, nk1 - 1)))]
    args = [x]
    if x2 is not None:
        in_specs.append(pl.BlockSpec((tm, tk), lambda i, j, k: (i, jnp.maximum(k - nk1, 0))))
        args.append(x2)
    in_specs.append(pl.BlockSpec((1, tk, tn), lambda i, j, k: (layer, k, j)))
    args.append(w)
    if epilogue == "resid":
        in_specs += [pl.BlockSpec((tm, tn), lambda i, j, k: (i, j)),
                     pl.BlockSpec((2, tn), lambda i, j, k: (0, j))]
        args += [res, gate]
    kernel = functools.partial(_mm_kernel, nk=nk, nk1=nk1, two_lhs=x2 is not None, epilogue=epilogue, n_lat=n_lat,
                               tm=tm)
    return pl.pallas_call(
        kernel,
        grid=(r // tm, n // tn, nk),
        in_specs=in_specs,
        out_specs=pl.BlockSpec((tm, tn), lambda i, j, k: (i, j)),
        out_shape=jax.ShapeDtypeStruct((r, n), out_dtype),
        scratch_shapes=[] if nk == 1 else [pltpu.VMEM((tm, tn), F32)],
        compiler_params=_cparams(("arbitrary", "arbitrary", "arbitrary")),
        name=name,
    )(*args)


def _mmw_kernel(*refs, prologue, two_lhs, w_rows, epilogue, n_lat, tm, k1, qk_tiles):
    refs = list(refs)
    x_ref = refs.pop(0)
    if prologue:
        nw_ref = refs.pop(0)
        mod_ref = refs.pop(0)
    x2_ref = refs.pop(0) if two_lhs else None
    w_ref = refs.pop(0)
    if epilogue == "resid":
        res_ref = refs.pop(0)
        gate_ref = refs.pop(0)
    if epilogue == "qknorm":
        qkw_ref = refs.pop(0)
    o_ref, wb_ref = refs[:2]
    j = pl.program_id(0)
    i = pl.program_id(1)

    @pl.when(i == 0)
    def _():
        wb_ref[...] = w_ref[0].astype(BF16)

    if prologue:
        h_ref = refs[2]
        for s in range(tm // ROW_TILE):
            rs = slice(s * ROW_TILE, (s + 1) * ROW_TILE)
            ctx_rows = i * tm + s * ROW_TILE >= n_lat
            scale = nw_ref[...] * (1.0 + jnp.where(ctx_rows, mod_ref[1, 0:1, :], mod_ref[0, 0:1, :]))
            shift = jnp.where(ctx_rows, mod_ref[1, 1:2, :], mod_ref[0, 1:2, :])
            x = x_ref[rs, :]
            y = x * lax.rsqrt(jnp.mean(x * x, axis=-1, keepdims=True) + RMS_EPS)
            h_ref[rs, :] = (y * scale + shift).astype(BF16)
        h = h_ref[...]
    else:
        h = x_ref[...]
    if two_lhs:
        acc = _dot(h, wb_ref[0:k1, :]) + _dot(x2_ref[...], wb_ref[k1:, :])
    elif w_rows:
        acc = _dot_nt(h, wb_ref[...])
    else:
        acc = _dot(h, wb_ref[...])
    if epilogue == "relu2":
        a = jnp.maximum(acc, 0.0)
        o_ref[...] = (a * a).astype(o_ref.dtype)
    elif epilogue == "resid":
        is_lat = (i * tm + lax.broadcasted_iota(jnp.int32, (tm, 1), 0)) < n_lat
        o_ref[...] = res_ref[...] + jnp.where(is_lat, gate_ref[0:1, :], gate_ref[1:2, :]) * acc
    elif epilogue == "qknorm":
        @pl.when(j < qk_tiles)
        def _():
            for b in range(acc.shape[1] // HEAD_DIM):
                sl = slice(b * HEAD_DIM, (b + 1) * HEAD_DIM)
                u = acc[:, sl]
                u = u * lax.rsqrt(jnp.mean(u * u, axis=-1, keepdims=True) + RMS_EPS) * qkw_ref[:, sl]
                o_ref[:, sl] = u.astype(o_ref.dtype)

        @pl.when(j >= qk_tiles)
        def _():
            o_ref[...] = acc.astype(o_ref.dtype)
    else:
        o_ref[...] = acc.astype(o_ref.dtype)


def _matmul_w(x, w, layer, *, norm=None, x2=None, rows=None, w_rows=False, col0=0, n_cols=None, tn=1024, out_dtype=F32,
              epilogue="none", res=None, gate=None, qk_w=None, n_lat=0, name="matmul_w"):
    r = x.shape[0] if rows is None else rows
    kdim, n_w = (w.shape[2], w.shape[1]) if w_rows else (w.shape[1], w.shape[2])
    n = n_w - col0 if n_cols is None else n_cols
    tn = min(tn, n)
    k1 = x.shape[1]
    tm = _pick(r, (1024, 768, 512, 256))
    in_specs = [pl.BlockSpec((tm, k1), lambda j, i: (i, 0))]
    args = [x]
    scratch = [pltpu.VMEM((tn, kdim) if w_rows else (kdim, tn), BF16)]
    if norm is not None:
        norm_w, scsh = norm
        in_specs += [pl.BlockSpec((1, k1), lambda j, i: (0, 0)),
                     pl.BlockSpec((2, 2, k1), lambda j, i: (0, 0, 0))]
        args += [norm_w.reshape(1, k1), scsh]
        scratch.append(pltpu.VMEM((tm, k1), BF16))
    if x2 is not None:
        in_specs.append(pl.BlockSpec((tm, x2.shape[1]), lambda j, i: (i, 0)))
        args.append(x2)
    if w_rows:
        in_specs.append(pl.BlockSpec((pl.Element(1), pl.Element(tn), pl.Element(kdim)),
                                     lambda j, i: (layer, pl.multiple_of(col0 + j * tn, SUBLANES), 0)))
    else:
        assert col0 == 0
        in_specs.append(pl.BlockSpec((1, kdim, tn), lambda j, i: (layer, 0, j)))
    args.append(w)
    if epilogue == "resid":
        in_specs += [pl.BlockSpec((tm, tn), lambda j, i: (i, j)),
                     pl.BlockSpec((2, tn), lambda j, i: (0, j))]
        args += [res, gate]
    qk_tiles = 0
    if epilogue == "qknorm":
        qk_tiles = qk_w.shape[0] // tn
        assert qk_w.shape[0] % tn == 0
        in_specs.append(pl.BlockSpec((1, tn), lambda j, i: (0, jnp.minimum(j, qk_tiles - 1))))
        args.append(qk_w.reshape(1, -1))
    kernel = functools.partial(_mmw_kernel, prologue=norm is not None, two_lhs=x2 is not None, w_rows=w_rows,
                               epilogue=epilogue, n_lat=n_lat, tm=tm, k1=k1, qk_tiles=qk_tiles)
    return pl.pallas_call(
        kernel,
        grid=(n // tn, r // tm),
        in_specs=in_specs,
        out_specs=pl.BlockSpec((tm, tn), lambda j, i: (i, j)),
        out_shape=jax.ShapeDtypeStruct((r, n), out_dtype),
        scratch_shapes=scratch,
        compiler_params=pltpu.CompilerParams(dimension_semantics=("arbitrary", "arbitrary"),
                                             vmem_limit_bytes=VMEM_LIMIT_WEIGHT_RESIDENT),
        name=name,
    )(*args)


def _prep_kernel(*refs, n_norm, n_rope, n_blocks, n_vt):
    refs = list(refs)
    p_ref, w_ref = refs[:2]
    if n_rope:
        cos = refs[2][...]
        sin = refs[3][...]
        even = (lax.broadcasted_iota(jnp.int32, cos.shape, 1) % 2) == 0
    o_ref = refs[-2] if n_vt else refs[-1]
    for b in range(n_blocks):
        sl = slice(b * HEAD_DIM, (b + 1) * HEAD_DIM)
        u = p_ref[:, sl]
        if b < n_norm:
            u = u * lax.rsqrt(jnp.mean(u * u, axis=-1, keepdims=True) + RMS_EPS) * w_ref[:, sl]
        if b < n_rope:
            partner = jnp.where(even, pltpu.roll(u, HEAD_DIM - 1, 1), pltpu.roll(u, 1, 1))
            u = u * cos + partner * sin
        if b < n_blocks - n_vt:
            o_ref[:, sl] = u.astype(o_ref.dtype)
        else:
            vt_ref = refs[-1]
            hv = b - (n_blocks - n_vt)
            vt_ref[hv, 0, 0:HEAD_DIM, :] = u.T.astype(BF16)
            pad_row = lax.broadcasted_iota(jnp.int32, (VT_ROWS - HEAD_DIM, u.shape[0]), 0)
            vt_ref[hv, 0, HEAD_DIM:, :] = jnp.where(pad_row == 0, 1.0, 0.0).astype(BF16)


def _attn_key_tile(n_all):
    for tk in (768, ROW_TILE):
        if n_all % tk == 0 and (n_all // tk) % 2 == 1:
            return tk
    raise ValueError(f"no key tile for {n_all} rows")


def _prep(p, wvec, n_norm, n_rope=0, cos=None, sin=None, n_vt=0):
    r, n = p.shape
    n_blocks = n // HEAD_DIM
    n_out = (n_blocks - n_vt) * HEAD_DIM
    tk = _attn_key_tile(r) if n_vt else ROW_TILE
    per = tk // ROW_TILE
    in_specs = [pl.BlockSpec((ROW_TILE, n), lambda i: (i, 0)),
                pl.BlockSpec((1, n), lambda i: (0, 0))]
    args = [p, wvec.reshape(1, n)]
    if n_rope:
        in_specs += [pl.BlockSpec((ROW_TILE, HEAD_DIM), lambda i: (i, 0))] * 2
        args += [cos, sin]
    out_specs = [pl.BlockSpec((ROW_TILE, n_out), lambda i: (i, 0))]
    out_shape = [jax.ShapeDtypeStruct((r, n_out), BF16)]
    if n_vt:
        out_specs.append(pl.BlockSpec((n_vt, 1, VT_ROWS, ROW_TILE), lambda i: (0, i // per, 0, i % per)))
        out_shape.append(jax.ShapeDtypeStruct((n_vt, r // tk, VT_ROWS, tk), BF16))
    out = pl.pallas_call(
        functools.partial(_prep_kernel, n_norm=n_norm, n_rope=n_rope, n_blocks=n_blocks, n_vt=n_vt),
        grid=(r // ROW_TILE,),
        in_specs=in_specs,
        out_specs=out_specs,
        out_shape=out_shape,
        compiler_params=_cparams(("arbitrary",)),
        name="qkv_prep",
    )(*args)
    return out if n_vt else out[0]


def _attn_kernel(q_ref, k_ref, vt_ref, o_ref, s0_ref, s1_ref, m_ref, acc_ref, *, group, tq, tk, n_lat, n_all):
    n_tiles = n_all // tk
    n_ctx = n_all - n_lat
    m_ref[...] = jnp.full(m_ref.shape, -jnp.inf, F32)
    acc_ref[...] = jnp.zeros(acc_ref.shape, F32)

    def scores(kc, s_ref, g):
        st = _dot_nt(kc, q_ref[:, g * HEAD_DIM:(g + 1) * HEAD_DIM])
        s_ref[g, 0:kc.shape[0], :] = st
        s_ref[g, tk:tk + 1, :] = jnp.max(st, axis=0, keepdims=True)

    def consume(vtc, s_ref, g):
        st = s_ref[g, 0:vtc.shape[1], :]
        m_prev = m_ref[g]
        m_new = jnp.maximum(m_prev, s_ref[g, tk:tk + 1, :])
        p = jnp.exp2(st - m_new).astype(BF16)
        acc_ref[g] = jnp.exp2(m_prev - m_new) * acc_ref[g] + _dot(vtc, p)
        m_ref[g] = m_new

    def key_tile(c):
        return k_ref[pl.ds(pl.multiple_of(c * tk, tk), tk), :]

    def step(k_next, vt_cur, s_next, s_cur):
        for g in range(group):
            if k_next is not None:
                scores(k_next, s_next, g)
            if vt_cur is not None:
                consume(vt_cur, s_cur, g)

    def pair(i, carry):
        c = 2 * i
        step(key_tile(c + 1), vt_ref[0, c], s1_ref, s0_ref)
        step(key_tile(c + 2), vt_ref[0, c + 1], s0_ref, s1_ref)
        return carry

    @pl.when(pl.program_id(1) < n_lat // tq)
    def _():
        step(key_tile(0), None, s0_ref, None)
        lax.fori_loop(0, (n_tiles - 1) // 2, pair, 0)
        step(None, vt_ref[0, n_tiles - 1], None, s0_ref)

    @pl.when(pl.program_id(1) >= n_lat // tq)
    def _():
        step(k_ref[n_lat:n_all, :], None, s0_ref, None)
        step(None, vt_ref[0, n_tiles - 1, :, tk - n_ctx:tk], None, s0_ref)

    for g in range(group):
        out = acc_ref[g, 0:HEAD_DIM, :] / acc_ref[g, HEAD_DIM:HEAD_DIM + 1, :]
        o_ref[:, g * HEAD_DIM:(g + 1) * HEAD_DIM] = out.T.astype(o_ref.dtype)


def _attention(qk, vt, *, n_heads, n_kv, n_lat, q_rows):
    n_all = qk.shape[0]
    group = n_heads // n_kv
    tq = ATTN_Q_TILE
    tk = vt.shape[-1]
    gw = group * HEAD_DIM
    assert n_all - n_lat <= min(tk, tq) and n_lat % tq == 0
    kernel = functools.partial(_attn_kernel, group=group, tq=tq, tk=tk, n_lat=n_lat, n_all=n_all)
    return pl.pallas_call(
        kernel,
        grid=(n_kv, pl.cdiv(q_rows, tq)),
        in_specs=[pl.BlockSpec((tq, gw), lambda h, i: (i, h)),
                  pl.BlockSpec((n_all, HEAD_DIM), lambda h, i: (0, n_heads + h)),
                  pl.BlockSpec((1, n_all // tk, VT_ROWS, tk), lambda h, i: (h, 0, 0, 0))],
        out_specs=pl.BlockSpec((tq, gw), lambda h, i: (i, h)),
        out_shape=jax.ShapeDtypeStruct((q_rows, n_heads * HEAD_DIM), BF16),
        scratch_shapes=[pltpu.VMEM((group, tk + SUBLANES, tq), F32),
                        pltpu.VMEM((group, tk + SUBLANES, tq), F32),
                        pltpu.VMEM((group, 1, tq), F32),
                        pltpu.VMEM((group, VT_ROWS, tq), F32)],
        compiler_params=_cparams(("arbitrary", "arbitrary")),
        name="gqa_attention",
    )(qk, qk, vt)


def _na_kernel(q_ref, k_ref, v_ref, bias_ref, o_ref, *, n_lat, rows, n_ctx):
    nb = rows // NA_QROWS
    b = pl.program_id(1)
    heads = [slice(j * HEAD_DIM, (j + 1) * HEAD_DIM) for j in range(NA_HEADS_PER_STEP)]
    s_c = [_dot_nt(q_ref[:, hs], k_ref[n_lat:n_lat + n_ctx, hs]) for hs in heads]

    @pl.when(b < nb)
    def _():
        w0 = jnp.clip(NA_QROWS * b - NA_ROWS // 2, 0, rows - NA_WROWS)
        win = pl.ds(pl.multiple_of(w0 * GRID_W, GRID_W), NA_WROWS * GRID_W)
        s_w = [_dot_nt(q_ref[:, hs], k_ref[win, hs]) + bias_ref[0, j] for j, hs in enumerate(heads)]
        m = [jnp.maximum(jnp.max(sw, axis=1, keepdims=True), jnp.max(sc, axis=1, keepdims=True))
             for sw, sc in zip(s_w, s_c)]
        p_w = [jnp.exp2(sw - mj) for sw, mj in zip(s_w, m)]
        p_c = [jnp.exp2(sc - mj) for sc, mj in zip(s_c, m)]
        for j, hs in enumerate(heads):
            l = jnp.sum(p_w[j], axis=1, keepdims=True) + jnp.sum(p_c[j], axis=1, keepdims=True)
            o = _dot(p_w[j].astype(BF16), v_ref[win, hs]) + _dot(p_c[j].astype(BF16), v_ref[n_lat:n_lat + n_ctx, hs])
            o_ref[:, hs] = (o / l).astype(o_ref.dtype)

    @pl.when(b >= nb)
    def _():
        for j, hs in enumerate(heads):
            p_c = jnp.exp2(s_c[j] - jnp.max(s_c[j], axis=1, keepdims=True))
            l = jnp.sum(p_c, axis=1, keepdims=True)
            o_ref[:, hs] = (_dot(p_c.astype(BF16), v_ref[n_lat:n_lat + n_ctx, hs]) / l).astype(o_ref.dtype)


def _na_bias_tiles(rpb, rows):
    h = rpb.shape[0]
    col = np.arange(GRID_W)
    c0 = np.clip(col - NA_COLS // 2, 0, GRID_W - NA_COLS)
    col_ok = (col[None, :] >= c0[:, None]) & (col[None, :] < c0[:, None] + NA_COLS)
    pad = GRID_W - NA_COLS
    rp = jnp.pad(rpb * math.log2(math.e), ((0, 0), (0, 0), (pad, pad)))
    toep = jnp.stack([rp[:, :, NA_COLS - 1 - q + pad:NA_COLS - 1 - q + pad + GRID_W] for q in range(GRID_W)], axis=2)
    toep = jnp.where(col_ok[None, None], toep, -jnp.inf)
    neg = jnp.full((h, GRID_W, GRID_W), -jnp.inf, F32)
    half = NA_ROWS // 2
    kinds = (lambda j: (j, 0), lambda j: (half + j, j), lambda j: (NA_ROWS + j, half))
    tiles = []
    for kind in kinds:
        qrows = []
        for j in range(NA_QROWS):
            q_row, win = kind(j)
            blocks = []
            for ik in range(NA_WROWS):
                dr = ik - q_row + (NA_ROWS - 1)
                ok = win <= ik < win + NA_ROWS
                blocks.append(toep[:, dr] if ok else neg)
            qrows.append(jnp.concatenate(blocks, axis=-1))
        tiles.append(jnp.concatenate(qrows, axis=-2))
    return jnp.stack(tiles, axis=0)


def _neighbourhood_attention(qkv, bias, *, n_heads, n_lat, n_ctx):
    r = qkv.shape[0]
    rows = n_lat // GRID_W
    nb = rows // NA_QROWS
    tq = NA_QROWS * GRID_W
    assert tq == n_ctx and rows >= NA_WROWS + NA_QROWS

    def bias_map(h, b):
        return (jnp.where(b == 0, 0, jnp.where(b >= nb - 1, 2, 1)), h, 0, 0)

    kernel = functools.partial(_na_kernel, n_lat=n_lat, rows=rows, n_ctx=n_ctx)
    hps = NA_HEADS_PER_STEP
    hw = hps * HEAD_DIM
    n_steps = n_heads // hps
    return pl.pallas_call(
        kernel,
        grid=(n_steps, r // tq),
        in_specs=[pl.BlockSpec((tq, hw), lambda h, b: (b, h)),
                  pl.BlockSpec((r, hw), lambda h, b: (0, n_steps + h)),
                  pl.BlockSpec((r, hw), lambda h, b: (0, 2 * n_steps + h)),
                  pl.BlockSpec((1, hps, tq, NA_WROWS * GRID_W), bias_map)],
        out_specs=pl.BlockSpec((tq, hw), lambda h, b: (b, h)),
        out_shape=jax.ShapeDtypeStruct((r, n_heads * HEAD_DIM), BF16),
        compiler_params=_cparams(("arbitrary", "arbitrary")),
        name="neighbourhood_attention",
    )(qkv, qkv, qkv, bias)


def _conv_kernel(u_ref, prev_ref, next_ref, w_ref, b_ref, o_ref, ext_ref, *, n_lat):
    tr = u_ref.shape[0]
    i = pl.program_id(0)
    nlb = n_lat // tr
    has_prev = jnp.logical_and(i != 0, i != nlb)
    has_next = i < nlb - 1
    ext_ref[0:SUBLANES, :] = jnp.where(has_prev, prev_ref[...], 0.0)
    ext_ref[SUBLANES:SUBLANES + tr, :] = u_ref[...]
    ext_ref[SUBLANES + tr:, :] = jnp.where(has_next, next_ref[...], 0.0)
    ext = ext_ref[...]
    n_ext = ext.shape[0]
    acc = jnp.broadcast_to(b_ref[...], o_ref.shape)
    for j in range(CONV_K):
        d = j - CONV_K // 2
        shifted = ext if d == 0 else pltpu.roll(ext, (n_ext - d) % n_ext, 0)
        acc = acc + w_ref[j:j + 1, :] * shifted[SUBLANES:SUBLANES + tr, :]
    o_ref[...] = (acc * _sigmoid(acc)).astype(o_ref.dtype)


def _conv_silu(proj, col0, conv_w, conv_b, ch0, n_ch, n_lat, out_dtype):
    r = proj.shape[0]
    ch = conv_w.shape[1]
    tr = ROW_TILE
    tc = _pick(n_ch, (1024, 512, 256, 128))
    assert (col0 + ch0) % tc == 0 and ch0 % tc == 0
    cb0 = (col0 + ch0) // tc
    wb0 = ch0 // tc
    rb = tr // SUBLANES
    last = r // SUBLANES - 1
    return pl.pallas_call(
        functools.partial(_conv_kernel, n_lat=n_lat),
        grid=(r // tr, n_ch // tc),
        in_specs=[pl.BlockSpec((tr, tc), lambda i, j: (i, cb0 + j)),
                  pl.BlockSpec((SUBLANES, tc), lambda i, j: (jnp.maximum(i * rb - 1, 0), cb0 + j)),
                  pl.BlockSpec((SUBLANES, tc), lambda i, j: (jnp.minimum((i + 1) * rb, last), cb0 + j)),
                  pl.BlockSpec((CONV_K, tc), lambda i, j: (0, wb0 + j)),
                  pl.BlockSpec((1, tc), lambda i, j: (0, wb0 + j))],
        out_specs=pl.BlockSpec((tr, tc), lambda i, j: (i, j)),
        out_shape=jax.ShapeDtypeStruct((r, n_ch), out_dtype),
        scratch_shapes=[pltpu.VMEM((tr + 2 * SUBLANES, tc), F32)],
        compiler_params=_cparams(("arbitrary", "arbitrary")),
        name="conv_silu",
    )(proj, proj, proj, conv_w, conv_b.reshape(1, ch))


def _ssd_kernel(xs_ref, b_ref, c_ref, dt_ref, dtt_ref, bias_ref, biast_ref, alog_ref, alogt_ref, e_ref,
                y_ref, st_ref, *, n_groups, heads_per_group):
    t = xs_ref.shape[0]
    d = pl.program_id(0)
    c = pl.program_id(1)
    gw = heads_per_group * SSM_HEAD_DIM

    @pl.when(c == 0)
    def _():
        st_ref[...] = jnp.zeros(st_ref.shape, F32)

    dt = _softplus(dt_ref[0] + bias_ref[0])
    dtt = _softplus(dtt_ref[0] + biast_ref[0])
    da = dt * (-jnp.exp(alog_ref[0]))
    dat = dtt * (-jnp.exp(alogt_ref[0]))
    row = lax.broadcasted_iota(jnp.int32, (t, t), 0)
    col = lax.broadcasted_iota(jnp.int32, (t, t), 1)
    lag = jnp.where(d == 0, row - col, col - row)
    valid = lag >= 0
    ones_v = jnp.where(valid, 1.0, 0.0).astype(BF16)
    ones_vt = jnp.where(lag <= 0, 1.0, 0.0).astype(BF16)
    cum = sum(_dot(ones_v, piece) for piece in _split3(da))
    cumt = sum(_dot(piece, ones_vt) for piece in _split3(dat))
    tot = jnp.sum(da, axis=0, keepdims=True)
    log2e = math.log2(math.e)
    cum_l2 = cum * log2e
    cumt_s2 = (cumt - jnp.log(dtt)) * log2e
    unreached = jnp.where(valid, 0.0, -jnp.inf)
    stack = jnp.concatenate([dt * jnp.exp(tot - cum), jnp.exp(cum),
                             jnp.broadcast_to(jnp.exp(tot), (SUBLANES, tot.shape[1]))], axis=0)
    wide = _dot(jnp.concatenate(_split3(stack), axis=1), e_ref[...])
    win_w = wide[0:t]
    ecum_w = wide[t:2 * t]
    cdec_w = wide[2 * t:2 * t + 1]
    xs = xs_ref[...]
    xin = (xs * win_w).astype(BF16)
    lane_head = lax.broadcasted_iota(jnp.int32, (t, gw), 1) // SSM_HEAD_DIM

    for g in range(n_groups):
        ns = slice(g * SSM_STATE, (g + 1) * SSM_STATE)
        hs = slice(g * gw, (g + 1) * gw)
        bg = b_ref[:, ns]
        cg = c_ref[:, ns]
        cb = _dot_nt(cg, bg)
        acc = _dot(cg, st_ref[g].astype(BF16)) * ecum_w[:, hs]
        xs_g = xs[:, hs].astype(BF16)
        for k in range(heads_per_group):
            h = g * heads_per_group + k
            decay = jnp.exp2((cum_l2[:, h:h + 1] - cumt_s2[h:h + 1, :]) + unreached)
            mh = (cb * decay).astype(BF16)
            xk = jnp.where(lane_head == k, xs_g, jnp.zeros_like(xs_g))
            acc = acc + _dot(mh, xk)
        y_ref[0, :, hs] = acc
        st_ref[g] = st_ref[g] * cdec_w[:, hs] + _dot(bg.astype(F32).T.astype(BF16), xin[:, hs])


def _ssd(xs, bc, dt_raw, dt_bias, a_log, n_lat):
    r, n_inner = xs.shape
    t = SSD_CHUNK
    heads = n_inner // SSM_HEAD_DIM
    gn = SSM_GROUPS * SSM_STATE
    hpg = heads // SSM_GROUPS
    n_lc = n_lat // t
    n_cc = (r - n_lat) // t
    assert (r - n_lat) % t == 0 and n_inner % gn == 0
    dt3 = dt_raw[:, :2 * heads].reshape(r, 2, heads).transpose(1, 0, 2)
    dt3t = dt3.transpose(0, 2, 1)
    expand = jnp.asarray(np.tile(np.repeat(np.eye(heads, dtype=np.float32), SSM_HEAD_DIM, axis=1), (3, 1)), BF16)

    def chunk(d, c):
        fwd = jnp.where(c < n_cc, n_lc + c, c - n_cc)
        bwd = n_lc + n_cc - 1 - c
        return jnp.where(d == 0, fwd, bwd)

    kernel = functools.partial(_ssd_kernel, n_groups=SSM_GROUPS, heads_per_group=hpg)
    return pl.pallas_call(
        kernel,
        grid=(2, n_lc + n_cc),
        in_specs=[pl.BlockSpec((t, n_inner), lambda d, c: (chunk(d, c), 0)),
                  pl.BlockSpec((t, gn), lambda d, c: (chunk(d, c), 0)),
                  pl.BlockSpec((t, gn), lambda d, c: (chunk(d, c), 1)),
                  pl.BlockSpec((1, t, heads), lambda d, c: (d, chunk(d, c), 0)),
                  pl.BlockSpec((1, heads, t), lambda d, c: (d, 0, chunk(d, c))),
                  pl.BlockSpec((1, 1, heads), lambda d, c: (d, 0, 0)),
                  pl.BlockSpec((1, heads, 1), lambda d, c: (d, 0, 0)),
                  pl.BlockSpec((1, 1, heads), lambda d, c: (d, 0, 0)),
                  pl.BlockSpec((1, heads, 1), lambda d, c: (d, 0, 0)),
                  pl.BlockSpec((3 * heads, n_inner), lambda d, c: (0, 0))],
        out_specs=pl.BlockSpec((1, t, n_inner), lambda d, c: (d, chunk(d, c), 0)),
        out_shape=jax.ShapeDtypeStruct((2, r, n_inner), F32),
        scratch_shapes=[pltpu.VMEM((SSM_GROUPS, SSM_STATE, hpg * SSM_HEAD_DIM), F32)],
        compiler_params=_cparams(("arbitrary", "arbitrary")),
        name="ssd_scan",
    )(xs, bc, bc, dt3, dt3t, dt_bias.reshape(2, 1, heads), dt_bias.reshape(2, heads, 1),
      a_log.reshape(2, 1, heads), a_log.reshape(2, heads, 1), expand)


def _gated_norm_kernel(y_ref, xs_ref, z_ref, dskip_ref, w_ref, o_ref, *, n_groups):
    z = z_ref[...]
    u = (y_ref[0] + y_ref[1] + dskip_ref[...] * xs_ref[...]) * (z * _sigmoid(z))
    gw = u.shape[1] // n_groups
    for g in range(n_groups):
        sl = slice(g * gw, (g + 1) * gw)
        ug = u[:, sl]
        ug = ug * lax.rsqrt(jnp.mean(ug * ug, axis=-1, keepdims=True) + RMS_EPS)
        o_ref[:, sl] = (ug * w_ref[:, sl]).astype(o_ref.dtype)


def _gated_norm(y, xs, proj, dskip_w, norm_w):
    _, r, n = y.shape
    return pl.pallas_call(
        functools.partial(_gated_norm_kernel, n_groups=SSM_NORM_GROUPS),
        grid=(r // ROW_TILE,),
        in_specs=[pl.BlockSpec((2, ROW_TILE, n), lambda i: (0, i, 0)),
                  pl.BlockSpec((ROW_TILE, n), lambda i: (i, 0)),
                  pl.BlockSpec((ROW_TILE, n), lambda i: (i, 0)),
                  pl.BlockSpec((1, n), lambda i: (0, 0)),
                  pl.BlockSpec((1, n), lambda i: (0, 0))],
        out_specs=pl.BlockSpec((ROW_TILE, n), lambda i: (i, 0)),
        out_shape=jax.ShapeDtypeStruct((r, n), BF16),
        compiler_params=_cparams(("arbitrary",)),
        name="gated_norm",
    )(y, xs, proj, dskip_w.reshape(1, n), norm_w.reshape(1, n))


def _rope_tables(n_lat, n_ctx):
    t = jnp.arange(n_lat)
    half = HEAD_DIM // 2
    inv = 1.0 / (ROPE_THETA ** (jnp.arange(0, half, 2, dtype=F32) / half))
    ang = jnp.concatenate([(t // GRID_W).astype(F32)[:, None] * inv,
                           (t % GRID_W).astype(F32)[:, None] * inv], axis=-1)
    cos = jnp.repeat(jnp.cos(ang), 2, axis=-1)
    sin = jnp.repeat(jnp.sin(ang), 2, axis=-1) * jnp.asarray(np.tile(np.array([-1.0, 1.0], np.float32), half))
    cos = jnp.concatenate([cos, jnp.ones((n_ctx, HEAD_DIM), F32)], axis=0)
    sin = jnp.concatenate([sin, jnp.zeros((n_ctx, HEAD_DIM), F32)], axis=0)
    return cos, sin


def kernel(x, c, ctx, c_ctx, w_mod, b_mod, norm1_w, norm2_w, w_mlp_in, w_mlp_out, ab_w_in, ab_conv_w, ab_conv_b,
           ab_dt_bias, ab_a_log, ab_d_skip, ab_norm_w, ab_q_norm, ab_k_norm, ab_rpb, ab_w_out, c_w_qkv, c_q_norm,
           c_k_norm, c_w_out):
    _, n_lat, d = x.shape
    n_ctx = ctx.shape[1]
    depth = w_mod.shape[0]
    assert x.shape[0] == 1 and n_ctx == ROW_TILE and n_lat % 1024 == 0
    n_inner = d
    ssm_heads = n_inner // SSM_HEAD_DIM
    conv_dim = ab_conv_w.shape[-1]
    na_heads = ab_rpb.shape[1]
    attn_heads = c_w_out.shape[1] // HEAD_DIM
    kv_heads = (c_w_qkv.shape[-1] // HEAD_DIM - attn_heads) // 2
    qk_scale = HEAD_DIM ** -0.5 * math.log2(math.e)
    i0 = n_inner
    i1 = i0 + conv_dim
    i2 = i1 + 2 * ssm_heads

    w_mlp_out_b = w_mlp_out.astype(BF16)
    ab_w_in_t = jnp.swapaxes(ab_w_in, 1, 2)

    xr = jnp.concatenate([x[0], ctx[0]], axis=0)
    vecs = jnp.zeros((SUBLANES, d), F32).at[0].set(c[0]).at[1].set(c_ctx)
    mods = _modulation(vecs, w_mod, b_mod)[:, :2].reshape(depth, 2, N_MOD, d)
    cos, sin = _rope_tables(n_lat, n_ctx)

    for layer in range(depth):
        last = layer == depth - 1
        i = layer // 2
        mod = mods[layer]
        scsh1 = jnp.stack([mod[:, 1], mod[:, 0]], axis=1)
        scsh2 = jnp.stack([mod[:, 4], mod[:, 3]], axis=1)
        rows = n_lat if last else n_lat + n_ctx
        if layer % 2 == 0:
            h = _normmod(xr, norm1_w[layer], scsh1, n_lat)
            proj = _matmul_w(h, ab_w_in_t, i, w_rows=True, n_cols=i1, name="ab_in_zxbc")
            dt_raw = _matmul_w(h, ab_w_in_t, i, w_rows=True, col0=i1, n_cols=i2 - i1, name="ab_in_dt")
            qk_w = jnp.concatenate([jnp.tile(ab_q_norm[i] * qk_scale, na_heads), jnp.tile(ab_k_norm[i], na_heads)])
            qkvb = _matmul_w(h, ab_w_in_t, i, w_rows=True, col0=i2, out_dtype=BF16, epilogue="qknorm", qk_w=qk_w,
                             name="ab_in_qkv")
            xs = _conv_silu(proj, i0, ab_conv_w[i], ab_conv_b[i], 0, n_inner, n_lat, F32)
            bc = _conv_silu(proj, i0, ab_conv_w[i], ab_conv_b[i], n_inner, conv_dim - n_inner, n_lat, BF16)
            y = _ssd(xs, bc, dt_raw, ab_dt_bias[i], ab_a_log[i], n_lat)
            dskip = jnp.repeat(ab_d_skip[i, 0] + ab_d_skip[i, 1], SSM_HEAD_DIM)
            gn = _gated_norm(y, xs, proj, dskip, ab_norm_w[i])
            bias = _na_bias_tiles(ab_rpb[i], n_lat // GRID_W)
            al = _neighbourhood_attention(qkvb, bias, n_heads=na_heads, n_lat=n_lat, n_ctx=n_ctx)
            xr = _matmul_w(gn, ab_w_out, i, x2=al, rows=rows, tn=512, epilogue="resid", res=xr, gate=mod[:, 2, :],
                           n_lat=n_lat, name="mixer_out")
        else:
            p = _matmul_w(xr, c_w_qkv, i, norm=(norm1_w[layer], scsh1), n_lat=n_lat, name="c_in_qkv")
            wvec = jnp.concatenate([jnp.tile(c_q_norm[i] * qk_scale, attn_heads), jnp.tile(c_k_norm[i], kv_heads),
                                    jnp.ones((kv_heads * HEAD_DIM,), F32)])
            qk, vt = _prep(p, wvec, attn_heads + kv_heads, attn_heads + kv_heads, cos, sin, n_vt=kv_heads)
            att = _attention(qk, vt, n_heads=attn_heads, n_kv=kv_heads, n_lat=n_lat, q_rows=rows)
            xr = _matmul_w(att, c_w_out, i, rows=rows, epilogue="resid", res=xr, gate=mod[:, 2, :], n_lat=n_lat,
                           name="mixer_out")
        a = _matmul_w(xr, w_mlp_in, layer, norm=(norm2_w[layer], scsh2), out_dtype=BF16, epilogue="relu2",
                      n_lat=n_lat, name="mlp_in")
        xr = _matmul(a, w_mlp_out_b, layer, epilogue="resid", res=xr, gate=mod[:, 5, :], n_lat=n_lat,
                     name="mlp_out")
    return xr[None]
```
